```python
import math
import jax, jax.numpy as jnp
from jax import lax
import numpy as np

D_MODEL = 1024
BATCH = 32
SEQ = 2048
DEPTH = 4
DEC_BATCH = 16
DEC_SEQ = 32
PAST_LEN = 1024

CHUNK = 64
N_MIXERS = 2
N_MLA_LAYERS = (DEPTH + 1) // 2
N_GM_LAYERS = DEPTH // 2

N_HEADS = 8
QK_NOPE = 128
QK_ROPE = 64
V_HEAD = 128
Q_LORA = 384
KV_LORA = 256
ROPE_THETA = 10000.0
Q_BLOCK = 128
ATTN_SCALE = (QK_NOPE + QK_ROPE) ** -0.5

GM_CHUNK = 128
GM_GROUPS = 8
GM_WIDTH = D_MODEL
GM_GROUP_DIM = GM_WIDTH // GM_GROUPS

D_FF = 2816
N_EXPERTS = 8
TOP_K = 2
D_FF_EXPERT = 3584

PLE_DIM = 256

ALPHA = (2 * DEPTH) ** 0.25
BETA = (8 * DEPTH) ** -0.25

kernel_name = "hybrid_mla_gmlp_streaming_step"


def layer_norm(x, g, b, eps=1e-5):
    xf = x.astype(jnp.float32)
    mu = jnp.mean(xf, -1, keepdims=True)
    var = jnp.mean(jnp.square(xf - mu), -1, keepdims=True)
    return ((xf - mu) * lax.rsqrt(var + eps)).astype(x.dtype) * g + b


def rms_norm(x, g, eps=1e-6):
    xf = x.astype(jnp.float32)
    return (xf * lax.rsqrt(jnp.mean(xf * xf, -1, keepdims=True) + eps)).astype(x.dtype) * g


def rope_tables(pos, dtype):
    half = QK_ROPE // 2
    inv = ROPE_THETA ** (-jnp.arange(half, dtype=jnp.float32) / half)
    ang = pos.astype(jnp.float32)[:, None] * inv[None, :]
    return jnp.cos(ang).astype(dtype), jnp.sin(ang).astype(dtype)


def apply_rope(x, cos, sin):
    half = QK_ROPE // 2
    x1, x2 = x[..., :half], x[..., half:]
    return jnp.concatenate([x1 * cos - x2 * sin, x1 * sin + x2 * cos], -1)


def mla_project(x, pos, w_in, g_q, w_q_up, g_kv):
    B, S, _ = x.shape
    h = x @ w_in
    c_q, c_kv, k_r = jnp.split(h, [Q_LORA, Q_LORA + KV_LORA], axis=-1)
    q = (rms_norm(c_q, g_q) @ w_q_up).reshape(B, S, N_HEADS, QK_NOPE + QK_ROPE)
    c_kv = rms_norm(c_kv, g_kv)
    cos, sin = rope_tables(pos, x.dtype)
    q_nope = q[..., :QK_NOPE]
    q_rope = apply_rope(q[..., QK_NOPE:], cos[:, None, :], sin[:, None, :])
    k_rope = apply_rope(k_r, cos, sin)
    return q_nope, q_rope, c_kv, k_rope


def mla_attention(q_nope, q_rope, q_start, c_all, kr_all, w_kv_up):
    B, S = q_nope.shape[:2]
    T = c_all.shape[1]
    kv = (c_all @ w_kv_up).reshape(B, T, N_HEADS, QK_NOPE + V_HEAD)
    k_nope, v = kv[..., :QK_NOPE], kv[..., QK_NOPE:]
    kv_chunk = jnp.arange(T) // CHUNK
    nb = S // Q_BLOCK if S % Q_BLOCK == 0 else 1
    qb = S // nb
    qn_b = q_nope.reshape(B, nb, qb, N_HEADS, QK_NOPE).transpose(1, 0, 2, 3, 4)
    qr_b = q_rope.reshape(B, nb, qb, N_HEADS, QK_ROPE).transpose(1, 0, 2, 3, 4)

    def block(args):
        qn, qr, i = args
        q_chunk = (q_start + i * qb + jnp.arange(qb)) // CHUNK
        mask = kv_chunk[None, :] <= q_chunk[:, None]
        s = (jnp.einsum('bqhd,bkhd->bhqk', qn, k_nope)
             + jnp.einsum('bqhr,bkr->bhqk', qr, kr_all)).astype(jnp.float32) * ATTN_SCALE
        s = jnp.where(mask, s, -jnp.inf)
        p = jax.nn.softmax(s, axis=-1).astype(v.dtype)
        return jnp.einsum('bhqk,bkhd->bqhd', p, v)

    o = lax.map(block, (qn_b, qr_b, jnp.arange(nb)))
    return o.transpose(1, 0, 2, 3, 4).reshape(B, S, N_HEADS * V_HEAD)


def gmlp_mix(x, w_in, ln_g, ln_b, w_s, b_s, w_out):
    B, S, _ = x.shape
    L = min(S, GM_CHUNK)
    nc = S // L
    h = jax.nn.gelu(x @ w_in)
    u, v = jnp.split(h, 2, axis=-1)
    v = layer_norm(v, ln_g, ln_b)
    wm = jnp.tril(w_s[:, :L, :L])
    vg = v.reshape(B, nc, L, GM_GROUPS, GM_GROUP_DIM)
    mixed = jnp.einsum('gts,bnsgc->bntgc', wm, vg) + b_s[:, :L].T[None, None, :, :, None]
    y = (u * mixed.reshape(B, S, GM_WIDTH)) @ w_out
    return y, v


def swiglu(x, wg, wu, wd):
    return (jax.nn.silu(x @ wg) * (x @ wu)) @ wd


def moe_swiglu(x, w_router, wg, wu, wd):
    probs = jax.nn.softmax((x @ w_router).astype(jnp.float32), axis=-1)
    top_p, top_i = lax.top_k(probs, TOP_K)
    top_p = top_p / jnp.sum(top_p, -1, keepdims=True)
    combine = jnp.sum(jax.nn.one_hot(top_i, N_EXPERTS, dtype=jnp.float32) * top_p[..., None], axis=-2)
    combine = combine.astype(x.dtype)
    y = jnp.zeros_like(x)
    for e in range(N_EXPERTS):
        y = y + combine[..., e:e + 1] * swiglu(x, wg[e], wu[e], wd[e])
    return y


def run_trunk(x, p, past_latent, past_krope, W):
    B, S, _ = x.shape
    past = 0 if past_latent is None else past_latent.shape[2]
    q_pos = past + jnp.arange(S)
    new_lat, new_kr, new_v = [], [], []
    for i in range(DEPTH):
        j = i // N_MIXERS
        if i % N_MIXERS == 0:
            qn, qr, c_kv, k_r = mla_project(x, q_pos, W['w_mla_in'][j], W['g_q_norm'][j],
                                            W['w_q_up'][j], W['g_kv_norm'][j])
            new_lat.append(c_kv)
            new_kr.append(k_r)
            if past_latent is None:
                c_all, kr_all = c_kv, k_r
            else:
                c_all = jnp.concatenate([past_latent[j], c_kv], axis=1)
                kr_all = jnp.concatenate([past_krope[j], k_r], axis=1)
            att = mla_attention(qn, qr, past, c_all, kr_all, W['w_kv_up'][j])
            h = att @ W['w_mla_out'][j]
        else:
            h, v_rows = gmlp_mix(x, W['w_gm_in'][j], W['g_gm_norm'][j], W['b_gm_norm'][j],
                                 W['w_gm_spatial'][j], W['b_gm_spatial'][j], W['w_gm_out'][j])
            new_v.append(v_rows[:, ((S - 1) // GM_CHUNK) * GM_CHUNK:])
        x = layer_norm(ALPHA * x + h, W['ln1_g'][i], W['ln1_b'][i])
        if i % 2 == 0:
            f = swiglu(x, W['w_ffn_gate'][j], W['w_ffn_up'][j], W['w_ffn_down'][j])
        else:
            f = moe_swiglu(x, W['w_router'][j], W['w_moe_gate'][j], W['w_moe_up'][j], W['w_moe_down'][j])
        x = layer_norm(ALPHA * x + f, W['ln2_g'][i], W['ln2_b'][i])
        x = x + jax.nn.sigmoid(x @ W['w_ple_gate'][i]) * (p[i] @ W['w_ple_proj'][i])
    return x, jnp.stack(new_lat), jnp.stack(new_kr), jnp.stack(new_v)


def setup_inputs(seed: int = 0) -> dict:
    key = jax.random.key(seed)
    ks = iter(list(jax.random.split(key, 48)))
    f32 = jnp.float32

    def nrm(shape, scale=1.0):
        return jax.random.normal(next(ks), shape, f32) * scale

    def gain(shape):
        return 1.0 + 0.02 * jax.random.normal(next(ks), shape, f32)

    nm, ng = N_MLA_LAYERS, N_GM_LAYERS
    return {
        'x_prompt': nrm((BATCH, SEQ, D_MODEL)),
        'x_sample': nrm((DEC_BATCH, DEC_SEQ, D_MODEL)),
        'cache_mla_latent': nrm((nm, DEC_BATCH, PAST_LEN, KV_LORA)),
        'cache_mla_krope': nrm((nm, DEC_BATCH, PAST_LEN, QK_ROPE)),
        'p_prompt': nrm((DEPTH, BATCH, SEQ, PLE_DIM)),
        'p_sample': nrm((DEPTH, DEC_BATCH, DEC_SEQ, PLE_DIM)),
        'w_mla_in': nrm((nm, D_MODEL, Q_LORA + KV_LORA + QK_ROPE), D_MODEL ** -0.5),
        'g_q_norm': gain((nm, Q_LORA)),
        'w_q_up': nrm((nm, Q_LORA, N_HEADS * (QK_NOPE + QK_ROPE)), Q_LORA ** -0.5),
        'g_kv_norm': gain((nm, KV_LORA)),
        'w_kv_up': nrm((nm, KV_LORA, N_HEADS * (QK_NOPE + V_HEAD)), KV_LORA ** -0.5),
        'w_mla_out': nrm((nm, N_HEADS * V_HEAD, D_MODEL), BETA * (N_HEADS * V_HEAD) ** -0.5),
        'w_gm_in': nrm((ng, D_MODEL, 2 * GM_WIDTH), D_MODEL ** -0.5),
        'g_gm_norm': gain((ng, GM_WIDTH)),
        'b_gm_norm': nrm((ng, GM_WIDTH), 0.02),
        'w_gm_spatial': nrm((ng, GM_GROUPS, GM_CHUNK, GM_CHUNK), GM_CHUNK ** -0.5),
        'b_gm_spatial': gain((ng, GM_GROUPS, GM_CHUNK)),
        'w_gm_out': nrm((ng, GM_WIDTH, D_MODEL), BETA * GM_WIDTH ** -0.5),
        'w_ffn_gate': nrm((nm, D_MODEL, D_FF), D_MODEL ** -0.5),
        'w_ffn_up': nrm((nm, D_MODEL, D_FF), D_MODEL ** -0.5),
        'w_ffn_down': nrm((nm, D_FF, D_MODEL), BETA * D_FF ** -0.5),
        'w_router': nrm((ng, D_MODEL, N_EXPERTS), D_MODEL ** -0.5),
        'w_moe_gate': nrm((ng, N_EXPERTS, D_MODEL, D_FF_EXPERT), D_MODEL ** -0.5),
        'w_moe_up': nrm((ng, N_EXPERTS, D_MODEL, D_FF_EXPERT), D_MODEL ** -0.5),
        'w_moe_down': nrm((ng, N_EXPERTS, D_FF_EXPERT, D_MODEL), BETA * D_FF_EXPERT ** -0.5),
        'ln1_g': gain((DEPTH, D_MODEL)),
        'ln1_b': nrm((DEPTH, D_MODEL), 0.02),
        'ln2_g': gain((DEPTH, D_MODEL)),
        'ln2_b': nrm((DEPTH, D_MODEL), 0.02),
        'w_ple_gate': nrm((DEPTH, D_MODEL, D_MODEL), D_MODEL ** -0.5),
        'w_ple_proj': nrm((DEPTH, PLE_DIM, D_MODEL), PLE_DIM ** -0.5),
    }


def reference(x_prompt, x_sample, cache_mla_latent, cache_mla_krope, p_prompt, p_sample,
              w_mla_in, g_q_norm, w_q_up, g_kv_norm, w_kv_up, w_mla_out,
              w_gm_in, g_gm_norm, b_gm_norm, w_gm_spatial, b_gm_spatial, w_gm_out,
              w_ffn_gate, w_ffn_up, w_ffn_down,
              w_router, w_moe_gate, w_moe_up, w_moe_down,
              ln1_g, ln1_b, ln2_g, ln2_b,
              w_ple_gate, w_ple_proj):
    W = dict(w_mla_in=w_mla_in, g_q_norm=g_q_norm, w_q_up=w_q_up, g_kv_norm=g_kv_norm,
             w_kv_up=w_kv_up, w_mla_out=w_mla_out,
             w_gm_in=w_gm_in, g_gm_norm=g_gm_norm, b_gm_norm=b_gm_norm,
             w_gm_spatial=w_gm_spatial, b_gm_spatial=b_gm_spatial, w_gm_out=w_gm_out,
             w_ffn_gate=w_ffn_gate, w_ffn_up=w_ffn_up, w_ffn_down=w_ffn_down,
             w_router=w_router, w_moe_gate=w_moe_gate, w_moe_up=w_moe_up, w_moe_down=w_moe_down,
             ln1_g=ln1_g, ln1_b=ln1_b, ln2_g=ln2_g, ln2_b=ln2_b,
             w_ple_gate=w_ple_gate, w_ple_proj=w_ple_proj)
    y_prompt, lat_p, kr_p, v_p = run_trunk(x_prompt, p_prompt, None, None, W)
    y_sample, lat_s, kr_s, v_s = run_trunk(x_sample, p_sample, cache_mla_latent, cache_mla_krope, W)
    return (y_prompt, y_sample, lat_p, kr_p, lat_s, kr_s, v_p, v_s)
```

```python
import functools
import math

import jax
import jax.numpy as jnp
from jax import lax
from jax.experimental import pallas as pl
from jax.experimental.pallas import tpu as pltpu

F32 = jnp.float32
BF16 = jnp.bfloat16

D_MODEL = 1024
DEPTH = 4
CHUNK = 64
N_HEADS = 8
QK_NOPE = 128
QK_ROPE = 64
V_HEAD = 128
Q_LORA = 384
KV_LORA = 256
ROPE_THETA = 10000.0
ATTN_SCALE = (QK_NOPE + QK_ROPE) ** -0.5
GM_CHUNK = 128
GM_GROUPS = 8
GM_GROUP_DIM = D_MODEL // GM_GROUPS
N_EXPERTS = 8
ALPHA = (2 * DEPTH) ** 0.25

LANES = 128
HEAD_PAD = 2 * LANES
ROPE_HALF = QK_ROPE // 2
LOG2E = 1.4426950408889634
Q_SCALE = ATTN_SCALE * LOG2E
GM_MIX = 256
VMEM_LIMIT = 56 * 1024 * 1024


def _params(*sem):
    return pltpu.CompilerParams(dimension_semantics=sem, vmem_limit_bytes=VMEM_LIMIT)


def _dot(a, b):
    return jnp.dot(a, b, preferred_element_type=F32)


def _layer_norm(z, g, b):
    mu = jnp.mean(z, -1, keepdims=True)
    zc = z - mu
    var = jnp.mean(zc * zc, -1, keepdims=True)
    return zc * lax.rsqrt(var + 1e-5) * g + b


def _rms_norm(z, g):
    return z * lax.rsqrt(jnp.mean(z * z, -1, keepdims=True) + 1e-6) * g


def _rope(r, cos_t, sin_t):
    return r * cos_t + pltpu.roll(r, ROPE_HALF, 1) * sin_t


def _mla_proj_kernel(x_ref, win_ref, gq_ref, wq_ref, gkv_ref, cos_ref, sin_ref,
                     q_ref, lat_ref, kr_ref, krp_ref):
    xb = x_ref[...].astype(BF16)
    h = _dot(xb, win_ref[...])
    cos_t = cos_ref[...]
    sin_t = sin_ref[...]
    cq = _rms_norm(h[:, :Q_LORA], gq_ref[...])
    q = _dot(cq.astype(BF16), wq_ref[...])
    for hd in range(N_HEADS):
        lo = hd * HEAD_PAD
        q_ref[:, lo:lo + LANES] = (q[:, lo:lo + LANES] * Q_SCALE).astype(BF16)
        rot = _rope(q[:, lo + LANES:lo + HEAD_PAD], cos_t, sin_t) * Q_SCALE
        q_ref[:, lo + LANES:lo + HEAD_PAD] = rot.astype(BF16)
    lat_ref[...] = _rms_norm(h[:, Q_LORA:Q_LORA + KV_LORA], gkv_ref[...])
    krot = _rope(h[:, Q_LORA + KV_LORA:], cos_t, sin_t)
    krp_ref[...] = krot
    kr_ref[...] = krot[:, :QK_ROPE]


def _mla_proj(x, w, cos_t, sin_t, tm):
    n = x.shape[0]
    s = cos_t.shape[0]
    per_seq = s // tm
    row = lambda i: (i, 0)
    const = lambda i: (0, 0)
    return pl.pallas_call(
        _mla_proj_kernel,
        grid=(n // tm,),
        in_specs=[
            pl.BlockSpec((tm, D_MODEL), row),
            pl.BlockSpec(w['w_in'].shape, const),
            pl.BlockSpec((1, Q_LORA), const),
            pl.BlockSpec(w['w_q_up'].shape, const),
            pl.BlockSpec((1, KV_LORA), const),
            pl.BlockSpec((tm, LANES), lambda i: (i % per_seq, 0)),
            pl.BlockSpec((tm, LANES), lambda i: (i % per_seq, 0)),
        ],
        out_specs=[
            pl.BlockSpec((tm, N_HEADS * HEAD_PAD), row),
            pl.BlockSpec((tm, KV_LORA), row),
            pl.BlockSpec((tm, QK_ROPE), row),
            pl.BlockSpec((tm, LANES), row),
        ],
        out_shape=[
            jax.ShapeDtypeStruct((n, N_HEADS * HEAD_PAD), BF16),
            jax.ShapeDtypeStruct((n, KV_LORA), F32),
            jax.ShapeDtypeStruct((n, QK_ROPE), F32),
            jax.ShapeDtypeStruct((n, LANES), F32),
        ],
        compiler_params=_params("parallel"),
        name="mla_proj",
    )(x, w['w_in'], w['g_q'], w['w_q_up'], w['g_kv'], cos_t, sin_t)


def _kv_up_kernel(lat_ref, krp_ref, wkv_ref, k_ref, v_ref):
    kv = _dot(lat_ref[...].astype(BF16), wkv_ref[...])
    krp = krp_ref[...].astype(BF16)
    for hd in range(N_HEADS):
        lo = hd * HEAD_PAD
        k_ref[:, lo:lo + LANES] = kv[:, hd * LANES:(hd + 1) * LANES].astype(BF16)
        k_ref[:, lo + LANES:lo + HEAD_PAD] = krp
    v_ref[...] = kv[:, N_HEADS * QK_NOPE:].astype(BF16)


def _kv_up(lat, krp, w_kv_up, tt):
    n = lat.shape[0]
    row = lambda i: (i, 0)
    return pl.pallas_call(
        _kv_up_kernel,
        grid=(n // tt,),
        in_specs=[
            pl.BlockSpec((tt, KV_LORA), row),
            pl.BlockSpec((tt, LANES), row),
            pl.BlockSpec(w_kv_up.shape, lambda i: (0, 0)),
        ],
        out_specs=[
            pl.BlockSpec((tt, N_HEADS * HEAD_PAD), row),
            pl.BlockSpec((tt, N_HEADS * V_HEAD), row),
        ],
        out_shape=[
            jax.ShapeDtypeStruct((n, N_HEADS * HEAD_PAD), BF16),
            jax.ShapeDtypeStruct((n, N_HEADS * V_HEAD), BF16),
        ],
        compiler_params=_params("parallel"),
        name="kv_up",
    )(lat, krp, w_kv_up)


def _scores(q, k):
    return lax.dot_general(q, k, (((1,), (1,)), ((), ())), preferred_element_type=F32)


def _attn_prefill_kernel(q_ref, k_ref, v_ref, o_ref, m_ref, l_ref, acc_ref, *, tq):
    i = pl.program_id(2)
    q = q_ref[...]
    m_ref[...] = jnp.full(m_ref.shape, -jnp.inf, F32)
    l_ref[...] = jnp.zeros(l_ref.shape, F32)
    acc_ref[...] = jnp.zeros(acc_ref.shape, F32)

    def step(j, masked):
        start = pl.multiple_of(j * tq, tq)
        s = _scores(q, k_ref[pl.ds(start, tq), :])
        if masked:
            q_chunk = lax.broadcasted_iota(jnp.int32, s.shape, 0) // CHUNK
            k_chunk = lax.broadcasted_iota(jnp.int32, s.shape, 1) // CHUNK
            s = jnp.where(k_chunk <= q_chunk, s, -jnp.inf)
        m_prev = m_ref[...]
        m_new = jnp.maximum(m_prev, jnp.max(s, axis=1, keepdims=True))
        p = jnp.exp2(s - m_new)
        alpha = jnp.exp2(m_prev - m_new)
        l_ref[...] = alpha * l_ref[...] + jnp.sum(p, axis=1, keepdims=True)
        acc_ref[...] = alpha * acc_ref[...] + _dot(p.astype(BF16), v_ref[pl.ds(start, tq), :])
        m_ref[...] = m_new

    def body(j, carry):
        step(j, False)
        return carry

    lax.fori_loop(0, i, body, 0)
    step(i, True)
    o_ref[...] = (acc_ref[...] / l_ref[...]).astype(o_ref.dtype)


def _attn_prefill(q, k, v, tq):
    b, s, _ = q.shape
    return pl.pallas_call(
        functools.partial(_attn_prefill_kernel, tq=tq),
        grid=(b, N_HEADS, s // tq),
        in_specs=[
            pl.BlockSpec((None, tq, HEAD_PAD), lambda b, h, i: (b, i, h)),
            pl.BlockSpec((None, s, HEAD_PAD), lambda b, h, i: (b, 0, h)),
            pl.BlockSpec((None, s, V_HEAD), lambda b, h, i: (b, 0, h)),
        ],
        out_specs=pl.BlockSpec((None, tq, V_HEAD), lambda b, h, i: (b, i, h)),
        out_shape=jax.ShapeDtypeStruct((b, s, N_HEADS * V_HEAD), BF16),
        scratch_shapes=[
            pltpu.VMEM((tq, 1), F32),
            pltpu.VMEM((tq, 1), F32),
            pltpu.VMEM((tq, V_HEAD), F32),
        ],
        compiler_params=_params("parallel", "parallel", "arbitrary"),
        name="attn_prefill",
    )(q, k, v)


def _attn_decode_kernel(q_ref, k_ref, v_ref, o_ref, *, q_start, t_valid):
    s = _scores(q_ref[...], k_ref[...])
    q_chunk = (q_start + lax.broadcasted_iota(jnp.int32, s.shape, 0)) // CHUNK
    k_pos = lax.broadcasted_iota(jnp.int32, s.shape, 1)
    s = jnp.where((k_pos // CHUNK <= q_chunk) & (k_pos < t_valid), s, -jnp.inf)
    p = jnp.exp2(s - jnp.max(s, axis=1, keepdims=True))
    l = jnp.sum(p, axis=1, keepdims=True)
    o_ref[...] = (_dot(p.astype(BF16), v_ref[...]) / l).astype(o_ref.dtype)


def _attn_decode(q, k, v, q_start, t_valid):
    b, s, _ = q.shape
    t = k.shape[1]
    return pl.pallas_call(
        functools.partial(_attn_decode_kernel, q_start=q_start, t_valid=t_valid),
        grid=(b, N_HEADS),
        in_specs=[
            pl.BlockSpec((None, s, HEAD_PAD), lambda b, h: (b, 0, h)),
            pl.BlockSpec((None, t, HEAD_PAD), lambda b, h: (b, 0, h)),
            pl.BlockSpec((None, t, V_HEAD), lambda b, h: (b, 0, h)),
        ],
        out_specs=pl.BlockSpec((None, s, V_HEAD), lambda b, h: (b, 0, h)),
        out_shape=jax.ShapeDtypeStruct((b, s, N_HEADS * V_HEAD), BF16),
        compiler_params=_params("parallel", "parallel"),
        name="attn_decode",
    )(q, k, v)


def _proj_ln_kernel(a_ref, x_ref, w_ref, g_ref, b_ref, o_ref):
    z = ALPHA * x_ref[...] + _dot(a_ref[...], w_ref[...])
    o_ref[...] = _layer_norm(z, g_ref[...], b_ref[...])


def _proj_ln(a, x, w, g, b, tm):
    n = x.shape[0]
    row = lambda i: (i, 0)
    const = lambda i: (0, 0)
    return pl.pallas_call(
        _proj_ln_kernel,
        grid=(n // tm,),
        in_specs=[
            pl.BlockSpec((tm, a.shape[1]), row),
            pl.BlockSpec((tm, D_MODEL), row),
            pl.BlockSpec(w.shape, const),
            pl.BlockSpec((1, D_MODEL), const),
            pl.BlockSpec((1, D_MODEL), const),
        ],
        out_specs=pl.BlockSpec((tm, D_MODEL), row),
        out_shape=jax.ShapeDtypeStruct((n, D_MODEL), F32),
        compiler_params=_params("parallel"),
        name="proj_ln",
    )(a, x, w, g, b)


def _gmlp_kernel(x_ref, win_ref, g_ref, b_ref, mix_ref, bias_ref, wout_ref, ln_g_ref, ln_b_ref,
                 o_ref, v_ref, *, tm, v_rows, v_every):
    x = x_ref[...]
    h = jax.nn.gelu(_dot(x.astype(BF16), win_ref[...]))
    u = h[:, :D_MODEL]
    v = _layer_norm(h[:, D_MODEL:], g_ref[...], b_ref[...])
    vb = v.astype(BF16)
    bias = bias_ref[...]
    gated = []
    for r in range(tm // GM_MIX):
        rows = slice(r * GM_MIX, (r + 1) * GM_MIX)
        cols = []
        for grp in range(GM_GROUPS):
            lanes = slice(grp * GM_GROUP_DIM, (grp + 1) * GM_GROUP_DIM)
            cols.append(_dot(mix_ref[grp], vb[rows, lanes]))
        mixed = jnp.concatenate(cols, axis=1) + bias
        gated.append((u[rows] * mixed).astype(BF16))
    y = _dot(jnp.concatenate(gated, axis=0), wout_ref[...])
    o_ref[...] = _layer_norm(ALPHA * x + y, ln_g_ref[...], ln_b_ref[...])

    @pl.when(pl.program_id(0) % v_every == v_every - 1)
    def _():
        v_ref[...] = v[tm - v_rows:, :]


def _gmlp(x, w, mix, bias, ln_g, ln_b, tm, v_rows, v_every):
    n = x.shape[0]
    row = lambda i: (i, 0)
    const = lambda i: (0, 0)
    n_v = n // (tm * v_every) * v_rows
    return pl.pallas_call(
        functools.partial(_gmlp_kernel, tm=tm, v_rows=v_rows, v_every=v_every),
        grid=(n // tm,),
        in_specs=[
            pl.BlockSpec((tm, D_MODEL), row),
            pl.BlockSpec(w['w_in'].shape, const),
            pl.BlockSpec((1, D_MODEL), const),
            pl.BlockSpec((1, D_MODEL), const),
            pl.BlockSpec(mix.shape, lambda i: (0, 0, 0)),
            pl.BlockSpec(bias.shape, const),
            pl.BlockSpec(w['w_out'].shape, const),
            pl.BlockSpec((1, D_MODEL), const),
            pl.BlockSpec((1, D_MODEL), const),
        ],
        out_specs=[
            pl.BlockSpec((tm, D_MODEL), row),
            pl.BlockSpec((v_rows, D_MODEL), lambda i: (i // v_every, 0)),
        ],
        out_shape=[
            jax.ShapeDtypeStruct((n, D_MODEL), F32),
            jax.ShapeDtypeStruct((n_v, D_MODEL), F32),
        ],
        compiler_params=_params("arbitrary"),
        name="gmlp",
    )(x, w['w_in'], w['g_norm'], w['b_norm'], mix, bias, w['w_out'], ln_g, ln_b)


def _router_kernel(x_ref, w_ref, o_ref):
    logits = jnp.dot(x_ref[...], w_ref[...], preferred_element_type=F32, precision=lax.Precision.HIGHEST)
    lane = lax.broadcasted_iota(jnp.int32, logits.shape, 1)
    logits = jnp.where(lane < N_EXPERTS, logits, -jnp.inf)
    m1 = jnp.max(logits, axis=1, keepdims=True)
    i1 = jnp.min(jnp.where(logits == m1, lane, LANES), axis=1, keepdims=True)
    rest = jnp.where(lane == i1, -jnp.inf, logits)
    m2 = jnp.max(rest, axis=1, keepdims=True)
    i2 = jnp.min(jnp.where(rest == m2, lane, LANES), axis=1, keepdims=True)
    e2 = jnp.exp(m2 - m1)
    w1 = 1.0 / (1.0 + e2)
    o_ref[...] = jnp.where(lane == i1, w1, jnp.where(lane == i2, e2 * w1, 0.0))


def _router(x, w_router, tm):
    n = x.shape[0]
    return pl.pallas_call(
        _router_kernel,
        grid=(n // tm,),
        in_specs=[
            pl.BlockSpec((tm, D_MODEL), lambda i: (i, 0)),
            pl.BlockSpec(w_router.shape, lambda i: (0, 0)),
        ],
        out_specs=pl.BlockSpec((tm, LANES), lambda i: (i, 0)),
        out_shape=jax.ShapeDtypeStruct((n, LANES), F32),
        compiler_params=_params("parallel"),
        name="router",
    )(x, w_router)


def _ffn_kernel(*refs, n_experts):
    if n_experts:
        x_ref, comb_ref, wg_ref, wu_ref, wd_ref, g_ref, b_ref, p_ref, wpg_ref, wpp_ref, o_ref, acc_ref, xb_ref = refs
        e, f = pl.program_id(1), pl.program_id(2)
        first = (e == 0) & (f == 0)
        last = (e == n_experts - 1) & (f == pl.num_programs(2) - 1)
    else:
        x_ref, wg_ref, wu_ref, wd_ref, g_ref, b_ref, p_ref, wpg_ref, wpp_ref, o_ref, acc_ref, xb_ref = refs
        f = pl.program_id(1)
        first = f == 0
        last = f == pl.num_programs(1) - 1

    @pl.when(first)
    def _():
        acc_ref[...] = jnp.zeros(acc_ref.shape, F32)
        xb_ref[...] = x_ref[...].astype(BF16)

    xb = xb_ref[...]
    h = jax.nn.silu(_dot(xb, wg_ref[...])) * _dot(xb, wu_ref[...])
    if n_experts:
        comb = comb_ref[...]
        lane = lax.broadcasted_iota(jnp.int32, comb.shape, 1)
        h = h * jnp.sum(jnp.where(lane == e, comb, 0.0), axis=1, keepdims=True)
    acc_ref[...] += _dot(h.astype(BF16), wd_ref[...])

    @pl.when(last)
    def _():
        x2 = _layer_norm(ALPHA * x_ref[...] + acc_ref[...], g_ref[...], b_ref[...])
        gate = jax.nn.sigmoid(_dot(x2.astype(BF16), wpg_ref[...]))
        o_ref[...] = x2 + gate * _dot(p_ref[...].astype(BF16), wpp_ref[...])


def _ffn(x, comb, wg, wu, wd, ln_g, ln_b, p, w_ple_gate, w_ple_proj, tm, tf):
    n = x.shape[0]
    n_experts = 0 if comb is None else wg.shape[0]
    d_ff = wg.shape[-1]
    if n_experts:
        grid = (n // tm, n_experts, d_ff // tf)
        row = lambda i, e, f: (i, 0)
        const = lambda i, e, f: (0, 0)
        w_specs = [
            pl.BlockSpec((None, D_MODEL, tf), lambda i, e, f: (e, 0, f)),
            pl.BlockSpec((None, D_MODEL, tf), lambda i, e, f: (e, 0, f)),
            pl.BlockSpec((None, tf, D_MODEL), lambda i, e, f: (e, f, 0)),
        ]
        sem = ("parallel", "arbitrary", "arbitrary")
    else:
        grid = (n // tm, d_ff // tf)
        row = lambda i, f: (i, 0)
        const = lambda i, f: (0, 0)
        w_specs = [
            pl.BlockSpec((D_MODEL, tf), lambda i, f: (0, f)),
            pl.BlockSpec((D_MODEL, tf), lambda i, f: (0, f)),
            pl.BlockSpec((tf, D_MODEL), lambda i, f: (f, 0)),
        ]
        sem = ("parallel", "arbitrary")
    in_specs = [pl.BlockSpec((tm, D_MODEL), row)]
    args = [x]
    if n_experts:
        in_specs.append(pl.BlockSpec((tm, LANES), row))
        args.append(comb)
    in_specs += w_specs + [
        pl.BlockSpec((1, D_MODEL), const),
        pl.BlockSpec((1, D_MODEL), const),
        pl.BlockSpec((tm, p.shape[1]), row),
        pl.BlockSpec(w_ple_gate.shape, const),
        pl.BlockSpec(w_ple_proj.shape, const),
    ]
    args += [wg, wu, wd, ln_g, ln_b, p, w_ple_gate, w_ple_proj]
    return pl.pallas_call(
        functools.partial(_ffn_kernel, n_experts=n_experts),
        grid=grid,
        in_specs=in_specs,
        out_specs=pl.BlockSpec((tm, D_MODEL), row),
        out_shape=jax.ShapeDtypeStruct((n, D_MODEL), F32),
        scratch_shapes=[pltpu.VMEM((tm, D_MODEL), F32), pltpu.VMEM((tm, D_MODEL), BF16)],
        compiler_params=_params(*sem),
        name="moe_ffn" if n_experts else "ffn",
    )(*args)


def _rope_tables(pos):
    inv = ROPE_THETA ** (-jnp.arange(ROPE_HALF, dtype=F32) / ROPE_HALF)
    ang = pos.astype(F32)[:, None] * inv[None, :]
    cos, sin, zero = jnp.cos(ang), jnp.sin(ang), jnp.zeros_like(ang)
    return (jnp.concatenate([cos, cos, zero, zero], axis=1),
            jnp.concatenate([-sin, sin, zero, zero], axis=1))


def _prep_mla(w_in, g_q, w_q_up, g_kv, w_kv_up, w_out):
    k_r = w_in[:, Q_LORA + KV_LORA:]
    w_in_p = jnp.concatenate([w_in[:, :Q_LORA + KV_LORA], k_r, k_r], axis=1)
    wq = w_q_up.reshape(Q_LORA, N_HEADS, QK_NOPE + QK_ROPE)
    wq_p = jnp.concatenate([wq, wq[:, :, QK_NOPE:]], axis=2).reshape(Q_LORA, N_HEADS * HEAD_PAD)
    wkv = w_kv_up.reshape(KV_LORA, N_HEADS, QK_NOPE + V_HEAD)
    wkv_p = jnp.concatenate([wkv[:, :, :QK_NOPE].reshape(KV_LORA, -1), wkv[:, :, QK_NOPE:].reshape(KV_LORA, -1)],
                            axis=1)
    return dict(w_in=w_in_p.astype(BF16), g_q=g_q[None, :], w_q_up=wq_p.astype(BF16), g_kv=g_kv[None, :],
                w_kv_up=wkv_p.astype(BF16), w_out=w_out.astype(BF16))


def _prep_gmlp_mix(w_s, b_s, chunk):
    wm = jnp.tril(w_s[:, :chunk, :chunk])
    reps = GM_MIX // chunk
    mix = jnp.einsum('ab,gts->gatbs', jnp.eye(reps, dtype=F32), wm).reshape(GM_GROUPS, GM_MIX, GM_MIX)
    bias = jnp.repeat(jnp.tile(b_s[:, :chunk].T, (reps, 1)), GM_GROUP_DIM, axis=1)
    return mix.astype(BF16), bias


def _row(v):
    return v[None, :]


def _trunk(x, p, past_latent, past_krope, W):
    b, s, _ = x.shape
    n = b * s
    tm = min(512, n)
    past = 0 if past_latent is None else past_latent.shape[2]
    cos_t, sin_t = _rope_tables(past + jnp.arange(s))
    if s < tm:
        cos_t, sin_t = jnp.tile(cos_t, (tm // s, 1)), jnp.tile(sin_t, (tm // s, 1))
    x = x.reshape(n, D_MODEL)
    chunk = min(s, GM_CHUNK)
    new_lat, new_kr, new_v = [], [], []
    for i in range(DEPTH):
        j = i // 2
        if i % 2 == 0:
            w = W['mla'][j]
            q, lat, kr, krp = _mla_proj(x, w, cos_t, sin_t, tm)
            new_lat.append(lat.reshape(b, s, KV_LORA))
            new_kr.append(kr.reshape(b, s, QK_ROPE))
            if past_latent is None:
                k, v = _kv_up(lat, krp, w['w_kv_up'], tm)
                att = _attn_prefill(q.reshape(b, s, -1), k.reshape(b, s, -1), v.reshape(b, s, -1), min(512, s))
            else:
                t_valid = past + s
                t_pad = -(-t_valid // LANES) * LANES
                lat_all = jnp.concatenate(
                    [past_latent[j], lat.reshape(b, s, KV_LORA), jnp.zeros((b, t_pad - t_valid, KV_LORA), F32)], axis=1)
                krp_all = jnp.concatenate(
                    [jnp.pad(past_krope[j], ((0, 0), (0, 0), (0, LANES - QK_ROPE))), krp.reshape(b, s, LANES),
                     jnp.zeros((b, t_pad - t_valid, LANES), F32)], axis=1)
                k, v = _kv_up(lat_all.reshape(b * t_pad, KV_LORA), krp_all.reshape(b * t_pad, LANES),
                              w['w_kv_up'], t_pad)
                att = _attn_decode(q.reshape(b, s, -1), k.reshape(b, t_pad, -1), v.reshape(b, t_pad, -1),
                                   past, t_valid)
            x = _proj_ln(att.reshape(n, -1), x, w['w_out'], _row(W['ln1_g'][i]), _row(W['ln1_b'][i]), tm)
            x = _ffn(x, None, W['w_ffn_gate'][j], W['w_ffn_up'][j], W['w_ffn_down'][j],
                     _row(W['ln2_g'][i]), _row(W['ln2_b'][i]), p[i].reshape(n, -1),
                     W['w_ple_gate'][i], W['w_ple_proj'][i], tm, 1408)
        else:
            w = W['gm'][j]
            mix, bias = _prep_gmlp_mix(w['w_spatial'], w['b_spatial'], chunk)
            if s >= GM_CHUNK:
                x, v_rows = _gmlp(x, w, mix, bias, _row(W['ln1_g'][i]), _row(W['ln1_b'][i]), tm, GM_CHUNK, s // tm)
                new_v.append(v_rows.reshape(b, GM_CHUNK, D_MODEL))
            else:
                x, v_rows = _gmlp(x, w, mix, bias, _row(W['ln1_g'][i]), _row(W['ln1_b'][i]), tm, tm, 1)
                new_v.append(v_rows.reshape(b, s, D_MODEL))
            comb = _router(x, W['w_router'][j], tm)
            x = _ffn(x, comb, W['w_moe_gate'][j], W['w_moe_up'][j], W['w_moe_down'][j],
                     _row(W['ln2_g'][i]), _row(W['ln2_b'][i]), p[i].reshape(n, -1),
                     W['w_ple_gate'][i], W['w_ple_proj'][i], tm, 896)
    return x.reshape(b, s, D_MODEL), jnp.stack(new_lat), jnp.stack(new_kr), jnp.stack(new_v)


def kernel(x_prompt, x_sample, cache_mla_latent, cache_mla_krope, p_prompt, p_sample, w_mla_in, g_q_norm, w_q_up, g_kv_norm, w_kv_up, w_mla_out, w_gm_in, g_gm_norm, b_gm_norm, w_gm_spatial, b_gm_spatial, w_gm_out, w_ffn_gate, w_ffn_up, w_ffn_down, w_router, w_moe_gate, w_moe_up, w_moe_down, ln1_g, ln1_b, ln2_g, ln2_b, w_ple_gate, w_ple_proj):
    n_mla, n_gm = w_mla_in.shape[0], w_gm_in.shape[0]
    W = dict(
        mla=[_prep_mla(w_mla_in[j], g_q_norm[j], w_q_up[j], g_kv_norm[j], w_kv_up[j], w_mla_out[j])
             for j in range(n_mla)],
        gm=[dict(w_in=w_gm_in[j].astype(BF16), g_norm=_row(g_gm_norm[j]), b_norm=_row(b_gm_norm[j]),
                 w_spatial=w_gm_spatial[j], b_spatial=b_gm_spatial[j], w_out=w_gm_out[j].astype(BF16))
            for j in range(n_gm)],
        w_ffn_gate=w_ffn_gate.astype(BF16), w_ffn_up=w_ffn_up.astype(BF16), w_ffn_down=w_ffn_down.astype(BF16),
        w_router=jnp.pad(w_router, ((0, 0), (0, 0), (0, LANES - N_EXPERTS))),
        w_moe_gate=w_moe_gate.astype(BF16), w_moe_up=w_moe_up.astype(BF16), w_moe_down=w_moe_down.astype(BF16),
        ln1_g=ln1_g, ln1_b=ln1_b, ln2_g=ln2_g, ln2_b=ln2_b,
        w_ple_gate=w_ple_gate.astype(BF16), w_ple_proj=w_ple_proj.astype(BF16),
    )
    y_p, lat_p, kr_p, v_p = _trunk(x_prompt, p_prompt, None, None, W)
    y_s, lat_s, kr_s, v_s = _trunk(x_sample, p_sample, cache_mla_latent, cache_mla_krope, W)
    return (y_p, y_s, lat_p, kr_p, lat_s, kr_s, v_p, v_s)
```

```python
import functools

import jax
import jax.numpy as jnp
from jax import lax
from jax.experimental import pallas as pl
from jax.experimental.pallas import tpu as pltpu

F32 = jnp.float32
BF16 = jnp.bfloat16

D_MODEL = 1024
DEPTH = 4
CHUNK = 64
N_HEADS = 8
QK_NOPE = 128
QK_ROPE = 64
V_HEAD = 128
Q_LORA = 384
KV_LORA = 256
ROPE_THETA = 10000.0
ATTN_SCALE = (QK_NOPE + QK_ROPE) ** -0.5
GM_CHUNK = 128
GM_GROUPS = 8
GM_GROUP_DIM = D_MODEL // GM_GROUPS
N_EXPERTS = 8
TOP_K = 2
ALPHA = (2 * DEPTH) ** 0.25

LANES = 128
BF16_ROWS = 16
HEAD_PAD = 2 * LANES
ROPE_HALF = QK_ROPE // 2
LOG2E = 1.4426950408889634
Q_SCALE = ATTN_SCALE * LOG2E
ATT_TILE = 256
GM_MIX = 256
TOKEN_TILE = 512
MOE_CHUNK = 128
MOE_SLOTS = TOP_K * TOKEN_TILE // MOE_CHUNK + N_EXPERTS
MOE_STAGE = MOE_SLOTS * MOE_CHUNK
MOE_GROUP = 512
VMEM_LIMIT = 56 * 1024 * 1024


def _params(*sem):
    return pltpu.CompilerParams(dimension_semantics=sem, vmem_limit_bytes=VMEM_LIMIT)


def _dot(a, b):
    return jnp.dot(a, b, preferred_element_type=F32)


def _layer_norm(z, g, b):
    mu = jnp.mean(z, -1, keepdims=True)
    zc = z - mu
    var = jnp.mean(zc * zc, -1, keepdims=True)
    return zc * lax.rsqrt(var + 1e-5) * g + b


def _rms_norm(z, g):
    return z * lax.rsqrt(jnp.mean(z * z, -1, keepdims=True) + 1e-6) * g


def _rope(r, cos_t, sin_t):
    return r * cos_t + pltpu.roll(r, ROPE_HALF, 1) * sin_t


def _mla_proj_kernel(x_ref, win_ref, gq_ref, wq_ref, gkv_ref, cos_ref, sin_ref,
                     q_ref, lat_ref, kr_ref, krp_ref):
    xb = x_ref[...].astype(BF16)
    h = _dot(xb, win_ref[...])
    cos_t = cos_ref[...]
    sin_t = sin_ref[...]
    cq = _rms_norm(h[:, :Q_LORA], gq_ref[...])
    q = _dot(cq.astype(BF16), wq_ref[...])
    for hd in range(N_HEADS):
        lo = hd * HEAD_PAD
        q_ref[:, lo:lo + LANES] = (q[:, lo:lo + LANES] * Q_SCALE).astype(BF16)
        rot = _rope(q[:, lo + LANES:lo + HEAD_PAD], cos_t, sin_t) * Q_SCALE
        q_ref[:, lo + LANES:lo + HEAD_PAD] = rot.astype(BF16)
    lat_ref[...] = _rms_norm(h[:, Q_LORA:Q_LORA + KV_LORA], gkv_ref[...])
    krot = _rope(h[:, Q_LORA + KV_LORA:], cos_t, sin_t)
    krp_ref[...] = krot
    kr_ref[...] = krot[:, :QK_ROPE]


def _mla_proj(x, w, cos_t, sin_t, tm):
    n = x.shape[0]
    per_seq = cos_t.shape[0] // tm
    row = lambda i: (i, 0)
    const = lambda i: (0, 0)
    return pl.pallas_call(
        _mla_proj_kernel,
        grid=(n // tm,),
        in_specs=[
            pl.BlockSpec((tm, D_MODEL), row),
            pl.BlockSpec(w['w_in'].shape, const),
            pl.BlockSpec((1, Q_LORA), const),
            pl.BlockSpec(w['w_q_up'].shape, const),
            pl.BlockSpec((1, KV_LORA), const),
            pl.BlockSpec((tm, LANES), lambda i: (i % per_seq, 0)),
            pl.BlockSpec((tm, LANES), lambda i: (i % per_seq, 0)),
        ],
        out_specs=[
            pl.BlockSpec((tm, N_HEADS * HEAD_PAD), row),
            pl.BlockSpec((tm, KV_LORA), row),
            pl.BlockSpec((tm, QK_ROPE), row),
            pl.BlockSpec((tm, LANES), row),
        ],
        out_shape=[
            jax.ShapeDtypeStruct((n, N_HEADS * HEAD_PAD), BF16),
            jax.ShapeDtypeStruct((n, KV_LORA), F32),
            jax.ShapeDtypeStruct((n, QK_ROPE), F32),
            jax.ShapeDtypeStruct((n, LANES), F32),
        ],
        compiler_params=_params("parallel"),
        name="mla_proj",
    )(x, w['w_in'], w['g_q'], w['w_q_up'], w['g_kv'], cos_t, sin_t)


def _kv_up_kernel(lat_ref, krp_ref, wk_ref, wv_ref, k_ref, v_ref, *, transpose_v):
    lat = lat_ref[...].astype(BF16)
    kn = _dot(lat, wk_ref[...])
    krp = krp_ref[...].astype(BF16)
    for hd in range(N_HEADS):
        lo = hd * HEAD_PAD
        k_ref[:, lo:lo + LANES] = kn[:, hd * LANES:(hd + 1) * LANES].astype(BF16)
        k_ref[:, lo + LANES:lo + HEAD_PAD] = krp
    if transpose_v:
        v_ref[...] = lax.dot_general(wv_ref[...], lat, (((1,), (1,)), ((), ())),
                                     preferred_element_type=F32).astype(BF16)
    else:
        v_ref[...] = _dot(lat, wv_ref[...]).astype(BF16)


def _kv_up(lat, krp, w, tt, transpose_v):
    b, t, _ = lat.shape
    wv = w['w_v_t'] if transpose_v else w['w_v']
    if transpose_v:
        v_spec = pl.BlockSpec((None, N_HEADS * V_HEAD, tt), lambda bi, i: (bi, 0, i))
        v_shape = jax.ShapeDtypeStruct((b, N_HEADS * V_HEAD, t), BF16)
    else:
        v_spec = pl.BlockSpec((None, tt, N_HEADS * V_HEAD), lambda bi, i: (bi, i, 0))
        v_shape = jax.ShapeDtypeStruct((b, t, N_HEADS * V_HEAD), BF16)
    return pl.pallas_call(
        functools.partial(_kv_up_kernel, transpose_v=transpose_v),
        grid=(b, t // tt),
        in_specs=[
            pl.BlockSpec((None, tt, KV_LORA), lambda bi, i: (bi, i, 0)),
            pl.BlockSpec((None, tt, LANES), lambda bi, i: (bi, i, 0)),
            pl.BlockSpec(w['w_k'].shape, lambda bi, i: (0, 0)),
            pl.BlockSpec(wv.shape, lambda bi, i: (0, 0)),
        ],
        out_specs=[pl.BlockSpec((None, tt, N_HEADS * HEAD_PAD), lambda bi, i: (bi, i, 0)), v_spec],
        out_shape=[jax.ShapeDtypeStruct((b, t, N_HEADS * HEAD_PAD), BF16), v_shape],
        compiler_params=_params("parallel", "parallel"),
        name="kv_up",
    )(lat, krp, w['w_k'], wv)


def _scores(a, b):
    return lax.dot_general(a, b, (((1,), (1,)), ((), ())), preferred_element_type=F32)


def _attn_prefill_kernel(q_ref, k_ref, vt_ref, o_ref):
    t = ATT_TILE
    k_chunk = lax.broadcasted_iota(jnp.int32, (t, t), 0) // CHUNK
    q_chunk = lax.broadcasted_iota(jnp.int32, (t, t), 1) // CHUNK
    diag_visible = k_chunk <= q_chunk
    for qi in range(q_ref.shape[0] // t):
        q = q_ref[qi * t:(qi + 1) * t, :]
        m = jnp.full((1, t), -jnp.inf, F32)
        l = jnp.zeros((1, t), F32)
        acc = jnp.zeros((V_HEAD, t), F32)
        for kj in range(qi + 1):
            s = _scores(k_ref[kj * t:(kj + 1) * t, :], q)
            if kj == qi:
                s = jnp.where(diag_visible, s, -jnp.inf)
            m_new = jnp.maximum(m, jnp.max(s, axis=0, keepdims=True))
            p = jnp.exp2(s - m_new)
            alpha = jnp.exp2(m - m_new)
            l = alpha * l + jnp.sum(p, axis=0, keepdims=True)
            acc = alpha * acc + _dot(vt_ref[:, kj * t:(kj + 1) * t], p.astype(BF16))
            m = m_new
        o_ref[qi * t:(qi + 1) * t, :] = (acc / l).T.astype(o_ref.dtype)


def _attn_prefill(q, k, vt):
    b, s, _ = q.shape
    return pl.pallas_call(
        _attn_prefill_kernel,
        grid=(b, N_HEADS),
        in_specs=[
            pl.BlockSpec((None, s, HEAD_PAD), lambda bi, h: (bi, 0, h)),
            pl.BlockSpec((None, s, HEAD_PAD), lambda bi, h: (bi, 0, h)),
            pl.BlockSpec((None, V_HEAD, s), lambda bi, h: (bi, h, 0)),
        ],
        out_specs=pl.BlockSpec((None, s, V_HEAD), lambda bi, h: (bi, 0, h)),
        out_shape=jax.ShapeDtypeStruct((b, s, N_HEADS * V_HEAD), BF16),
        compiler_params=_params("parallel", "parallel"),
        name="attn_prefill",
    )(q, k, vt)


def _attn_decode_kernel(q_ref, k_ref, v_ref, o_ref, *, q_start, t_valid):
    s = _scores(q_ref[...], k_ref[...])
    q_chunk = (q_start + lax.broadcasted_iota(jnp.int32, s.shape, 0)) // CHUNK
    k_pos = lax.broadcasted_iota(jnp.int32, s.shape, 1)
    s = jnp.where((k_pos // CHUNK <= q_chunk) & (k_pos < t_valid), s, -jnp.inf)
    p = jnp.exp2(s - jnp.max(s, axis=1, keepdims=True))
    l = jnp.sum(p, axis=1, keepdims=True)
    o_ref[...] = (_dot(p.astype(BF16), v_ref[...]) / l).astype(o_ref.dtype)


def _attn_decode(q, k, v, q_start, t_valid):
    b, s, _ = q.shape
    t = k.shape[1]
    return pl.pallas_call(
        functools.partial(_attn_decode_kernel, q_start=q_start, t_valid=t_valid),
        grid=(b, N_HEADS),
        in_specs=[
            pl.BlockSpec((None, s, HEAD_PAD), lambda bi, h: (bi, 0, h)),
            pl.BlockSpec((None, t, HEAD_PAD), lambda bi, h: (bi, 0, h)),
            pl.BlockSpec((None, t, V_HEAD), lambda bi, h: (bi, 0, h)),
        ],
        out_specs=pl.BlockSpec((None, s, V_HEAD), lambda bi, h: (bi, 0, h)),
        out_shape=jax.ShapeDtypeStruct((b, s, N_HEADS * V_HEAD), BF16),
        compiler_params=_params("parallel", "parallel"),
        name="attn_decode",
    )(q, k, v)


def _proj_ln_kernel(a_ref, x_ref, w_ref, g_ref, b_ref, o_ref):
    z = ALPHA * x_ref[...] + _dot(a_ref[...], w_ref[...])
    o_ref[...] = _layer_norm(z, g_ref[...], b_ref[...])


def _proj_ln(a, x, w, g, b, tm):
    n = x.shape[0]
    row = lambda i: (i, 0)
    const = lambda i: (0, 0)
    return pl.pallas_call(
        _proj_ln_kernel,
        grid=(n // tm,),
        in_specs=[
            pl.BlockSpec((tm, a.shape[1]), row),
            pl.BlockSpec((tm, D_MODEL), row),
            pl.BlockSpec(w.shape, const),
            pl.BlockSpec((1, D_MODEL), const),
            pl.BlockSpec((1, D_MODEL), const),
        ],
        out_specs=pl.BlockSpec((tm, D_MODEL), row),
        out_shape=jax.ShapeDtypeStruct((n, D_MODEL), F32),
        compiler_params=_params("parallel"),
        name="proj_ln",
    )(a, x, w, g, b)


def _gmlp_kernel(x_ref, win_ref, g_ref, b_ref, mix_ref, bias_ref, wout_ref, ln_g_ref, ln_b_ref,
                 o_ref, v_ref, *, tm, v_rows, v_every):
    x = x_ref[...]
    h = jax.nn.gelu(_dot(x.astype(BF16), win_ref[...]))
    u = h[:, :D_MODEL]
    v = _layer_norm(h[:, D_MODEL:], g_ref[...], b_ref[...])
    vb = v.astype(BF16)
    bias = bias_ref[...]
    gated = []
    for r in range(tm // GM_MIX):
        rows = slice(r * GM_MIX, (r + 1) * GM_MIX)
        cols = []
        for grp in range(GM_GROUPS):
            lanes = slice(grp * GM_GROUP_DIM, (grp + 1) * GM_GROUP_DIM)
            cols.append(_dot(mix_ref[grp], vb[rows, lanes]))
        mixed = jnp.concatenate(cols, axis=1) + bias
        gated.append((u[rows] * mixed).astype(BF16))
    y = _dot(jnp.concatenate(gated, axis=0), wout_ref[...])
    o_ref[...] = _layer_norm(ALPHA * x + y, ln_g_ref[...], ln_b_ref[...])

    @pl.when(pl.program_id(0) % v_every == v_every - 1)
    def _():
        v_ref[...] = v[tm - v_rows:, :]


def _gmlp(x, w, mix, bias, ln_g, ln_b, tm, v_rows, v_every):
    n = x.shape[0]
    row = lambda i: (i, 0)
    const = lambda i: (0, 0)
    n_v = n // (tm * v_every) * v_rows
    return pl.pallas_call(
        functools.partial(_gmlp_kernel, tm=tm, v_rows=v_rows, v_every=v_every),
        grid=(n // tm,),
        in_specs=[
            pl.BlockSpec((tm, D_MODEL), row),
            pl.BlockSpec(w['w_in'].shape, const),
            pl.BlockSpec((1, D_MODEL), const),
            pl.BlockSpec((1, D_MODEL), const),
            pl.BlockSpec(mix.shape, lambda i: (0, 0, 0)),
            pl.BlockSpec(bias.shape, const),
            pl.BlockSpec(w['w_out'].shape, const),
            pl.BlockSpec((1, D_MODEL), const),
            pl.BlockSpec((1, D_MODEL), const),
        ],
        out_specs=[
            pl.BlockSpec((tm, D_MODEL), row),
            pl.BlockSpec((v_rows, D_MODEL), lambda i: (i // v_every, 0)),
        ],
        out_shape=[
            jax.ShapeDtypeStruct((n, D_MODEL), F32),
            jax.ShapeDtypeStruct((n_v, D_MODEL), F32),
        ],
        compiler_params=_params("arbitrary"),
        name="gmlp",
    )(x, w['w_in'], w['g_norm'], w['b_norm'], mix, bias, w['w_out'], ln_g, ln_b)


def _ln_ple(z, g_ref, b_ref, p_ref, wpg_ref, wpp_ref):
    x2 = _layer_norm(z, g_ref[...], b_ref[...])
    gate = jax.nn.sigmoid(_dot(x2.astype(BF16), wpg_ref[...]))
    return x2 + gate * _dot(p_ref[...].astype(BF16), wpp_ref[...])


def _ffn_kernel(x_ref, wg_ref, wu_ref, wd_ref, g_ref, b_ref, p_ref, wpg_ref, wpp_ref, o_ref, acc_ref, xb_ref):
    f = pl.program_id(1)

    @pl.when(f == 0)
    def _():
        acc_ref[...] = jnp.zeros(acc_ref.shape, F32)
        xb_ref[...] = x_ref[...].astype(BF16)

    xb = xb_ref[...]
    h = jax.nn.silu(_dot(xb, wg_ref[...])) * _dot(xb, wu_ref[...])
    acc_ref[...] += _dot(h.astype(BF16), wd_ref[...])

    @pl.when(f == pl.num_programs(1) - 1)
    def _():
        o_ref[...] = _ln_ple(ALPHA * x_ref[...] + acc_ref[...], g_ref, b_ref, p_ref, wpg_ref, wpp_ref)


def _ffn(x, wg, wu, wd, ln_g, ln_b, p, w_ple_gate, w_ple_proj, tm, tf):
    n = x.shape[0]
    row = lambda i, f: (i, 0)
    const = lambda i, f: (0, 0)
    return pl.pallas_call(
        _ffn_kernel,
        grid=(n // tm, wg.shape[1] // tf),
        in_specs=[
            pl.BlockSpec((tm, D_MODEL), row),
            pl.BlockSpec((D_MODEL, tf), lambda i, f: (0, f)),
            pl.BlockSpec((D_MODEL, tf), lambda i, f: (0, f)),
            pl.BlockSpec((tf, D_MODEL), lambda i, f: (f, 0)),
            pl.BlockSpec((1, D_MODEL), const),
            pl.BlockSpec((1, D_MODEL), const),
            pl.BlockSpec((tm, p.shape[1]), row),
            pl.BlockSpec(w_ple_gate.shape, const),
            pl.BlockSpec(w_ple_proj.shape, const),
        ],
        out_specs=pl.BlockSpec((tm, D_MODEL), row),
        out_shape=jax.ShapeDtypeStruct((n, D_MODEL), F32),
        scratch_shapes=[pltpu.VMEM((tm, D_MODEL), F32), pltpu.VMEM((tm, D_MODEL), BF16)],
        compiler_params=_params("parallel", "arbitrary"),
        name="ffn",
    )(x, wg, wu, wd, ln_g, ln_b, p, w_ple_gate, w_ple_proj)


def _router_kernel(x_ref, w_ref, comb_ref, code_ref, cnt_ref):
    logits = jnp.dot(x_ref[...], w_ref[...], preferred_element_type=F32, precision=lax.Precision.HIGHEST)
    tm = logits.shape[0]
    lane = lax.broadcasted_iota(jnp.int32, logits.shape, 1)
    logits = jnp.where(lane < N_EXPERTS, logits, -jnp.inf)
    m1 = jnp.max(logits, axis=1, keepdims=True)
    i1 = jnp.min(jnp.where(logits == m1, lane, LANES), axis=1, keepdims=True)
    rest = jnp.where(lane == i1, -jnp.inf, logits)
    m2 = jnp.max(rest, axis=1, keepdims=True)
    i2 = jnp.min(jnp.where(rest == m2, lane, LANES), axis=1, keepdims=True)
    e2 = jnp.exp(m2 - m1)
    w1 = 1.0 / (1.0 + e2)
    comb_ref[...] = jnp.where(lane == i1, w1, jnp.where(lane == i2, e2 * w1, 0.0))
    chosen = (lane == i1) | (lane == i2)
    onehot = jnp.where(chosen, 1.0, 0.0)
    earlier = lax.broadcasted_iota(jnp.int32, (tm, tm), 0) > lax.broadcasted_iota(jnp.int32, (tm, tm), 1)
    rank = _dot(jnp.where(earlier, 1.0, 0.0).astype(BF16), onehot.astype(BF16))
    code_ref[...] = jnp.where(chosen, rank, -1.0)
    cnt_ref[...] = jnp.broadcast_to(jnp.sum(onehot, axis=0, keepdims=True), cnt_ref.shape)


def _router(x, w_router, tm):
    n = x.shape[0]
    row = lambda i: (i, 0)
    return pl.pallas_call(
        _router_kernel,
        grid=(n // tm,),
        in_specs=[
            pl.BlockSpec((tm, D_MODEL), row),
            pl.BlockSpec(w_router.shape, lambda i: (0, 0)),
        ],
        out_specs=[
            pl.BlockSpec((tm, LANES), row),
            pl.BlockSpec((tm, LANES), row),
            pl.BlockSpec((8, LANES), row),
        ],
        out_shape=[
            jax.ShapeDtypeStruct((n, LANES), F32),
            jax.ShapeDtypeStruct((n, LANES), F32),
            jax.ShapeDtypeStruct((n // tm * 8, LANES), F32),
        ],
        compiler_params=_params("parallel"),
        name="router",
    )(x, w_router)


def _moe_plan(cnt, n_tiles, tm_ffn, n_ffn_tiles):
    cnt = cnt.reshape(n_tiles, 8, LANES)[:, 0, :N_EXPERTS].astype(jnp.int32)
    nch = (cnt + MOE_CHUNK - 1) // MOE_CHUNK
    base = jnp.cumsum(nch, axis=1) - nch
    seg = (cnt + BF16_ROWS - 1) // BF16_ROWS * BF16_ROWS
    group_rows = jnp.sum(seg, axis=0) + MOE_CHUNK
    group_pad = (group_rows + tm_ffn - 1) // tm_ffn * tm_ffn
    group_start = jnp.cumsum(group_pad) - group_pad
    seg_start = group_start[None, :] + jnp.cumsum(seg, axis=0) - seg
    tile_end = jnp.cumsum(group_pad // tm_ffn)
    n_used = tile_end[-1]
    tile_id = jnp.minimum(jnp.arange(n_ffn_tiles), n_used - 1)
    tile_expert = jnp.sum(tile_id[:, None] >= tile_end[None, :], axis=1)
    base_rows = jnp.zeros((n_tiles, 8, LANES), F32).at[:, :, :N_EXPERTS].set(
        (base * MOE_CHUNK).astype(F32)[:, None, :]).reshape(n_tiles * 8, LANES)
    flat = lambda a: a.reshape(-1).astype(jnp.int32)
    return dict(seg=flat(seg_start), nch=flat(nch), base=flat(base), base_rows=base_rows,
                tile_expert=flat(tile_expert), n_used=flat(n_used))


def _staged_rows(code, base_rows):
    pos = jnp.where(code >= 0.0, base_rows + code, -1.0)
    first = jnp.max(pos, axis=1, keepdims=True)
    second = jnp.max(jnp.where(pos == first, -1.0, pos), axis=1, keepdims=True)
    return pos, first, second


def _chunk_copies(hbm_ref, vmem_ref, sem, seg_ref, nch_ref, base_ref, tile, slot, to_hbm, action):
    for e in range(N_EXPERTS):
        idx = tile * N_EXPERTS + e
        for c in range(TOKEN_TILE // MOE_CHUNK):
            @pl.when(c < nch_ref[idx])
            def _():
                stage = vmem_ref.at[slot, pl.ds(pl.multiple_of((base_ref[idx] + c) * MOE_CHUNK, MOE_CHUNK), MOE_CHUNK)]
                group = hbm_ref.at[pl.ds(pl.multiple_of(seg_ref[idx] + c * MOE_CHUNK, BF16_ROWS), MOE_CHUNK)]
                src, dst = (stage, group) if to_hbm else (group, stage)
                copy = pltpu.make_async_copy(src, dst, sem.at[slot])
                copy.start() if action == "start" else copy.wait()


def _dispatch_kernel(seg_ref, nch_ref, base_ref, x_ref, code_ref, brow_ref, zeros_ref, xs_ref, stage_ref, sem):
    del zeros_ref
    t = pl.program_id(0)
    slot = t % 2
    _, first, second = _staged_rows(code_ref[...], brow_ref[0:1, :])
    lane = lax.broadcasted_iota(jnp.int32, (TOKEN_TILE, LANES), 1)
    rows_t = jnp.where(lane == 0, first, jnp.where(lane == 1, second, -1.0)).T
    r1, r2 = rows_t[0:1, :], rows_t[1:2, :]
    xb = x_ref[...].astype(BF16)
    n_chunks = base_ref[t * N_EXPERTS + N_EXPERTS - 1] + nch_ref[t * N_EXPERTS + N_EXPERTS - 1]
    for g in range(MOE_STAGE // MOE_GROUP):
        @pl.when(g * (MOE_GROUP // MOE_CHUNK) < n_chunks)
        def _():
            row = (lax.broadcasted_iota(jnp.int32, (MOE_GROUP, TOKEN_TILE), 0) + g * MOE_GROUP).astype(F32)
            sel = jnp.where((row == r1) | (row == r2), 1.0, 0.0).astype(BF16)
            stage_ref[slot, g * MOE_GROUP:(g + 1) * MOE_GROUP, :] = _dot(sel, xb).astype(BF16)

    copies = functools.partial(_chunk_copies, xs_ref, stage_ref, sem, seg_ref, nch_ref, base_ref, to_hbm=True)

    @pl.when(t > 0)
    def _():
        copies(tile=t - 1, slot=1 - slot, action="wait")

    copies(tile=t, slot=slot, action="start")

    @pl.when(t == pl.num_programs(0) - 1)
    def _():
        copies(tile=t, slot=slot, action="wait")


def _dispatch(x, code, plan, n_rows):
    n = x.shape[0]
    row = lambda i, *_: (i, 0)
    grid_spec = pltpu.PrefetchScalarGridSpec(
        num_scalar_prefetch=3,
        grid=(n // TOKEN_TILE,),
        in_specs=[
            pl.BlockSpec((TOKEN_TILE, D_MODEL), row),
            pl.BlockSpec((TOKEN_TILE, LANES), row),
            pl.BlockSpec((8, LANES), row),
            pl.BlockSpec(memory_space=pl.ANY),
        ],
        out_specs=pl.BlockSpec(memory_space=pl.ANY),
        scratch_shapes=[pltpu.VMEM((2, MOE_STAGE, D_MODEL), BF16), pltpu.SemaphoreType.DMA((2,))],
    )
    return pl.pallas_call(
        _dispatch_kernel,
        grid_spec=grid_spec,
        out_shape=jax.ShapeDtypeStruct((n_rows, D_MODEL), BF16),
        input_output_aliases={6: 0},
        compiler_params=_params("arbitrary"),
        name="moe_dispatch",
    )(plan['seg'], plan['nch'], plan['base'], x, code, plan['base_rows'], jnp.zeros((n_rows, D_MODEL), BF16))


def _moe_ffn_kernel(te_ref, nu_ref, x_ref, wg_ref, wu_ref, wd_ref, o_ref, acc_ref):
    del te_ref
    f = pl.program_id(1)
    used = pl.program_id(0) < nu_ref[0]

    @pl.when(jnp.logical_not(used) & (f == 0))
    def _():
        o_ref[...] = jnp.zeros(o_ref.shape, o_ref.dtype)

    @pl.when(used)
    def _():
        @pl.when(f == 0)
        def _():
            acc_ref[...] = jnp.zeros(acc_ref.shape, F32)

        xb = x_ref[...]
        h = jax.nn.silu(_dot(xb, wg_ref[...])) * _dot(xb, wu_ref[...])
        acc_ref[...] += _dot(h.astype(BF16), wd_ref[...])

        @pl.when(f == pl.num_programs(1) - 1)
        def _():
            o_ref[...] = acc_ref[...].astype(o_ref.dtype)


def _moe_ffn(xs, plan, wg, wu, wd, tm, tf):
    n_rows = xs.shape[0]
    n_f = wg.shape[-1] // tf
    tile = lambda i, f, te, nu: (jnp.minimum(i, nu[0] - 1), 0)
    f_idx = lambda i, f, nu: jnp.where(i < nu[0], f, n_f - 1)
    grid_spec = pltpu.PrefetchScalarGridSpec(
        num_scalar_prefetch=2,
        grid=(n_rows // tm, n_f),
        in_specs=[
            pl.BlockSpec((tm, D_MODEL), tile),
            pl.BlockSpec((None, D_MODEL, tf), lambda i, f, te, nu: (te[i], 0, f_idx(i, f, nu))),
            pl.BlockSpec((None, D_MODEL, tf), lambda i, f, te, nu: (te[i], 0, f_idx(i, f, nu))),
            pl.BlockSpec((None, tf, D_MODEL), lambda i, f, te, nu: (te[i], f_idx(i, f, nu), 0)),
        ],
        out_specs=pl.BlockSpec((tm, D_MODEL), lambda i, f, te, nu: (i, 0)),
        scratch_shapes=[pltpu.VMEM((tm, D_MODEL), F32)],
    )
    return pl.pallas_call(
        _moe_ffn_kernel,
        grid_spec=grid_spec,
        out_shape=jax.ShapeDtypeStruct((n_rows, D_MODEL), BF16),
        compiler_params=_params("arbitrary", "arbitrary"),
        name="moe_ffn",
    )(plan['tile_expert'], plan['n_used'], xs, wg, wu, wd)


def _combine_kernel(seg_ref, nch_ref, base_ref, x_ref, comb_ref, code_ref, brow_ref, ys_ref,
                    g_ref, b_ref, p_ref, wpg_ref, wpp_ref, o_ref, buf_ref, y_ref, sem):
    t = pl.program_id(0)
    slot = t % 2
    copies = functools.partial(_chunk_copies, ys_ref, buf_ref, sem, seg_ref, nch_ref, base_ref, to_hbm=False)

    @pl.when(t == 0)
    def _():
        buf_ref[...] = jnp.zeros(buf_ref.shape, BF16)
        copies(tile=t, slot=slot, action="start")

    @pl.when(t + 1 < pl.num_programs(0))
    def _():
        copies(tile=t + 1, slot=1 - slot, action="start")

    copies(tile=t, slot=slot, action="wait")

    comb = comb_ref[...]
    pos, first, second = _staged_rows(code_ref[...], brow_ref[0:1, :])
    w_first = jnp.max(jnp.where(pos == first, comb, 0.0), axis=1, keepdims=True)
    w_second = jnp.max(jnp.where(pos == second, comb, 0.0), axis=1, keepdims=True)
    y_ref[...] = jnp.zeros(y_ref.shape, F32)
    n_chunks = base_ref[t * N_EXPERTS + N_EXPERTS - 1] + nch_ref[t * N_EXPERTS + N_EXPERTS - 1]
    for g in range(MOE_STAGE // MOE_GROUP):
        @pl.when(g * (MOE_GROUP // MOE_CHUNK) < n_chunks)
        def _():
            row = (lax.broadcasted_iota(jnp.int32, (TOKEN_TILE, MOE_GROUP), 1) + g * MOE_GROUP).astype(F32)
            rows = buf_ref[slot, g * MOE_GROUP:(g + 1) * MOE_GROUP, :]
            y_ref[...] += (w_first * _dot(jnp.where(row == first, 1.0, 0.0).astype(BF16), rows)
                           + w_second * _dot(jnp.where(row == second, 1.0, 0.0).astype(BF16), rows))

    o_ref[...] = _ln_ple(ALPHA * x_ref[...] + y_ref[...], g_ref, b_ref, p_ref, wpg_ref, wpp_ref)


def _combine(x, comb, code, plan, ys, ln_g, ln_b, p, w_ple_gate, w_ple_proj):
    n = x.shape[0]
    row = lambda i, *_: (i, 0)
    const = lambda i, *_: (0, 0)
    grid_spec = pltpu.PrefetchScalarGridSpec(
        num_scalar_prefetch=3,
        grid=(n // TOKEN_TILE,),
        in_specs=[
            pl.BlockSpec((TOKEN_TILE, D_MODEL), row),
            pl.BlockSpec((TOKEN_TILE, LANES), row),
            pl.BlockSpec((TOKEN_TILE, LANES), row),
            pl.BlockSpec((8, LANES), row),
            pl.BlockSpec(memory_space=pl.ANY),
            pl.BlockSpec((1, D_MODEL), const),
            pl.BlockSpec((1, D_MODEL), const),
            pl.BlockSpec((TOKEN_TILE, p.shape[1]), row),
            pl.BlockSpec(w_ple_gate.shape, const),
            pl.BlockSpec(w_ple_proj.shape, const),
        ],
        out_specs=pl.BlockSpec((TOKEN_TILE, D_MODEL), row),
        scratch_shapes=[pltpu.VMEM((2, MOE_STAGE, D_MODEL), BF16), pltpu.VMEM((TOKEN_TILE, D_MODEL), F32),
                        pltpu.SemaphoreType.DMA((2,))],
    )
    return pl.pallas_call(
        _combine_kernel,
        grid_spec=grid_spec,
        out_shape=jax.ShapeDtypeStruct((n, D_MODEL), F32),
        compiler_params=_params("arbitrary"),
        name="moe_combine",
    )(plan['seg'], plan['nch'], plan['base'], x, comb, code, plan['base_rows'], ys,
      ln_g, ln_b, p, w_ple_gate, w_ple_proj)


def _moe(x, w_router, wg, wu, wd, ln_g, ln_b, p, w_ple_gate, w_ple_proj):
    n = x.shape[0]
    n_tiles = n // TOKEN_TILE
    tm_ffn = 1024 if n >= 8192 else 256
    max_rows = TOP_K * n + n_tiles * N_EXPERTS * (BF16_ROWS - 1) + N_EXPERTS * (MOE_CHUNK + tm_ffn - 1)
    n_ffn_tiles = -(-max_rows // tm_ffn)
    comb, code, cnt = _router(x, w_router, TOKEN_TILE)
    plan = _moe_plan(cnt, n_tiles, tm_ffn, n_ffn_tiles)
    xs = _dispatch(x, code, plan, n_ffn_tiles * tm_ffn)
    ys = _moe_ffn(xs, plan, wg, wu, wd, tm_ffn, 896)
    return _combine(x, comb, code, plan, ys, ln_g, ln_b, p, w_ple_gate, w_ple_proj)


def _rope_tables(pos):
    inv = ROPE_THETA ** (-jnp.arange(ROPE_HALF, dtype=F32) / ROPE_HALF)
    ang = pos.astype(F32)[:, None] * inv[None, :]
    cos, sin, zero = jnp.cos(ang), jnp.sin(ang), jnp.zeros_like(ang)
    return (jnp.concatenate([cos, cos, zero, zero], axis=1),
            jnp.concatenate([-sin, sin, zero, zero], axis=1))


def _prep_mla(w_in, g_q, w_q_up, g_kv, w_kv_up, w_out):
    k_r = w_in[:, Q_LORA + KV_LORA:]
    w_in_p = jnp.concatenate([w_in[:, :Q_LORA + KV_LORA], k_r, k_r], axis=1)
    wq = w_q_up.reshape(Q_LORA, N_HEADS, QK_NOPE + QK_ROPE)
    wq_p = jnp.concatenate([wq, wq[:, :, QK_NOPE:]], axis=2).reshape(Q_LORA, N_HEADS * HEAD_PAD)
    wkv = w_kv_up.reshape(KV_LORA, N_HEADS, QK_NOPE + V_HEAD)
    w_k = wkv[:, :, :QK_NOPE].reshape(KV_LORA, -1).astype(BF16)
    w_v = wkv[:, :, QK_NOPE:].reshape(KV_LORA, -1).astype(BF16)
    return dict(w_in=w_in_p.astype(BF16), g_q=g_q[None, :], w_q_up=wq_p.astype(BF16), g_kv=g_kv[None, :],
                w_k=w_k, w_v=w_v, w_v_t=w_v.T, w_out=w_out.astype(BF16))


def _prep_gmlp_mix(w_s, b_s, chunk):
    wm = jnp.tril(w_s[:, :chunk, :chunk])
    reps = GM_MIX // chunk
    mix = jnp.einsum('ab,gts->gatbs', jnp.eye(reps, dtype=F32), wm).reshape(GM_GROUPS, GM_MIX, GM_MIX)
    bias = jnp.repeat(jnp.tile(b_s[:, :chunk].T, (reps, 1)), GM_GROUP_DIM, axis=1)
    return mix.astype(BF16), bias


def _row(v):
    return v[None, :]


def _trunk(x, p, past_latent, past_krope, W):
    b, s, _ = x.shape
    n = b * s
    tm = TOKEN_TILE
    past = 0 if past_latent is None else past_latent.shape[2]
    cos_t, sin_t = _rope_tables(past + jnp.arange(s))
    if s < tm:
        cos_t, sin_t = jnp.tile(cos_t, (tm // s, 1)), jnp.tile(sin_t, (tm // s, 1))
    x = x.reshape(n, D_MODEL)
    chunk = min(s, GM_CHUNK)
    new_lat, new_kr, new_v = [], [], []
    for i in range(DEPTH):
        j = i // 2
        ln1 = (_row(W['ln1_g'][i]), _row(W['ln1_b'][i]))
        tail = (_row(W['ln2_g'][i]), _row(W['ln2_b'][i]), p[i].reshape(n, -1), W['w_ple_gate'][i], W['w_ple_proj'][i])
        if i % 2 == 0:
            w = W['mla'][j]
            q, lat, kr, krp = _mla_proj(x, w, cos_t, sin_t, tm)
            lat = lat.reshape(b, s, KV_LORA)
            krp = krp.reshape(b, s, LANES)
            new_lat.append(lat)
            new_kr.append(kr.reshape(b, s, QK_ROPE))
            q = q.reshape(b, s, -1)
            if past_latent is None:
                k, vt = _kv_up(lat, krp, w, tm, True)
                att = _attn_prefill(q, k, vt)
            else:
                t_valid = past + s
                t_pad = -(-t_valid // LANES) * LANES
                lat_all = jnp.concatenate(
                    [past_latent[j], lat, jnp.zeros((b, t_pad - t_valid, KV_LORA), F32)], axis=1)
                krp_all = jnp.concatenate(
                    [jnp.pad(past_krope[j], ((0, 0), (0, 0), (0, LANES - QK_ROPE))), krp,
                     jnp.zeros((b, t_pad - t_valid, LANES), F32)], axis=1)
                k, v = _kv_up(lat_all, krp_all, w, t_pad, False)
                att = _attn_decode(q, k, v, past, t_valid)
            x = _proj_ln(att.reshape(n, -1), x, w['w_out'], *ln1, tm)
            x = _ffn(x, W['w_ffn_gate'][j], W['w_ffn_up'][j], W['w_ffn_down'][j], *tail, tm, 1408)
        else:
            w = W['gm'][j]
            mix, bias = _prep_gmlp_mix(w['w_spatial'], w['b_spatial'], chunk)
            if s >= GM_CHUNK:
                x, v_rows = _gmlp(x, w, mix, bias, *ln1, tm, GM_CHUNK, s // tm)
                new_v.append(v_rows.reshape(b, GM_CHUNK, D_MODEL))
            else:
                x, v_rows = _gmlp(x, w, mix, bias, *ln1, tm, tm, 1)
                new_v.append(v_rows.reshape(b, s, D_MODEL))
            x = _moe(x, W['w_router'][j], W['w_moe_gate'][j], W['w_moe_up'][j], W['w_moe_down'][j], *tail)
    return x.reshape(b, s, D_MODEL), jnp.stack(new_lat), jnp.stack(new_kr), jnp.stack(new_v)


def kernel(x_prompt, x_sample, cache_mla_latent, cache_mla_krope, p_prompt, p_sample, w_mla_in, g_q_norm, w_q_up, g_kv_norm, w_kv_up, w_mla_out, w_gm_in, g_gm_norm, b_gm_norm, w_gm_spatial, b_gm_spatial, w_gm_out, w_ffn_gate, w_ffn_up, w_ffn_down, w_router, w_moe_gate, w_moe_up, w_moe_down, ln1_g, ln1_b, ln2_g, ln2_b, w_ple_gate, w_ple_proj):
    n_mla, n_gm = w_mla_in.shape[0], w_gm_in.shape[0]
    W = dict(
        mla=[_prep_mla(w_mla_in[j], g_q_norm[j], w_q_up[j], g_kv_norm[j], w_kv_up[j], w_mla_out[j])
             for j in range(n_mla)],
        gm=[dict(w_in=w_gm_in[j].astype(BF16), g_norm=_row(g_gm_norm[j]), b_norm=_row(b_gm_norm[j]),
                 w_spatial=w_gm_spatial[j], b_spatial=b_gm_spatial[j], w_out=w_gm_out[j].astype(BF16))
            for j in range(n_gm)],
        w_ffn_gate=w_ffn_gate.astype(BF16), w_ffn_up=w_ffn_up.astype(BF16), w_ffn_down=w_ffn_down.astype(BF16),
        w_router=jnp.pad(w_router, ((0, 0), (0, 0), (0, LANES - N_EXPERTS))),
        w_moe_gate=w_moe_gate.astype(BF16), w_moe_up=w_moe_up.astype(BF16), w_moe_down=w_moe_down.astype(BF16),
        ln1_g=ln1_g, ln1_b=ln1_b, ln2_g=ln2_g, ln2_b=ln2_b,
        w_ple_gate=w_ple_gate.astype(BF16), w_ple_proj=w_ple_proj.astype(BF16),
    )
    y_p, lat_p, kr_p, v_p = _trunk(x_prompt, p_prompt, None, None, W)
    y_s, lat_s, kr_s, v_s = _trunk(x_sample, p_sample, cache_mla_latent, cache_mla_krope, W)
    return (y_p, y_s, lat_p, kr_p, lat_s, kr_s, v_p, v_s)
```

```python
import functools

import jax
import jax.numpy as jnp
from jax import lax
from jax.experimental import pallas as pl
from jax.experimental.pallas import tpu as pltpu

F32 = jnp.float32
BF16 = jnp.bfloat16

D_MODEL = 1024
DEPTH = 4
CHUNK = 64
N_HEADS = 8
QK_NOPE = 128
QK_ROPE = 64
V_HEAD = 128
Q_LORA = 384
KV_LORA = 256
ROPE_THETA = 10000.0
ATTN_SCALE = (QK_NOPE + QK_ROPE) ** -0.5
GM_CHUNK = 128
GM_GROUPS = 8
GM_GROUP_DIM = D_MODEL // GM_GROUPS
N_EXPERTS = 8
TOP_K = 2
ALPHA = (2 * DEPTH) ** 0.25

LANES = 128
BF16_ROWS = 16
HEAD_PAD = 2 * LANES
ROPE_HALF = QK_ROPE // 2
LOG2E = 1.4426950408889634
Q_SCALE = ATTN_SCALE * LOG2E
ATT_TILE = 256
ATT_ROWS = 32
GM_MIX = 256
FF_CHUNK = 1024
MOE_FF_BLOCK = 1792
TOKEN_TILE = 512
MOE_CHUNK = 128
MOE_SLOTS = TOP_K * TOKEN_TILE // MOE_CHUNK + N_EXPERTS
MOE_STAGE = MOE_SLOTS * MOE_CHUNK
MOE_GROUP = 512
VMEM_LIMIT = 56 * 1024 * 1024


def _params(*sem):
    return pltpu.CompilerParams(dimension_semantics=sem, vmem_limit_bytes=VMEM_LIMIT)


def _dot(a, b):
    return jnp.dot(a, b, preferred_element_type=F32)


def _layer_norm(z, g, b):
    mu = jnp.mean(z, -1, keepdims=True)
    zc = z - mu
    var = jnp.mean(zc * zc, -1, keepdims=True)
    return zc * lax.rsqrt(var + 1e-5) * g + b


def _rms_norm(z, g):
    return z * lax.rsqrt(jnp.mean(z * z, -1, keepdims=True) + 1e-6) * g


def _rope(r, cos_t, sin_t):
    return r * cos_t + pltpu.roll(r, ROPE_HALF, 1) * sin_t


def _mla_proj_kernel(x_ref, win_ref, gq_ref, wq_ref, gkv_ref, cos_ref, sin_ref,
                     q_ref, lat_ref, kr_ref, krp_ref):
    xb = x_ref[...].astype(BF16)
    h = _dot(xb, win_ref[...])
    cos_t = cos_ref[...]
    sin_t = sin_ref[...]
    cq = _rms_norm(h[:, :Q_LORA], gq_ref[...])
    q = _dot(cq.astype(BF16), wq_ref[...])
    for hd in range(N_HEADS):
        lo = hd * HEAD_PAD
        q_ref[:, lo:lo + LANES] = (q[:, lo:lo + LANES] * Q_SCALE).astype(BF16)
        rot = _rope(q[:, lo + LANES:lo + HEAD_PAD], cos_t, sin_t) * Q_SCALE
        q_ref[:, lo + LANES:lo + HEAD_PAD] = rot.astype(BF16)
    lat_ref[...] = _rms_norm(h[:, Q_LORA:Q_LORA + KV_LORA], gkv_ref[...])
    krot = _rope(h[:, Q_LORA + KV_LORA:], cos_t, sin_t)
    krp_ref[...] = krot
    kr_ref[...] = krot[:, :QK_ROPE]


def _mla_proj(x, w, cos_t, sin_t, tm):
    n = x.shape[0]
    per_seq = cos_t.shape[0] // tm
    row = lambda i: (i, 0)
    const = lambda i: (0, 0)
    return pl.pallas_call(
        _mla_proj_kernel,
        grid=(n // tm,),
        in_specs=[
            pl.BlockSpec((tm, D_MODEL), row),
            pl.BlockSpec(w['w_in'].shape, const),
            pl.BlockSpec((1, Q_LORA), const),
            pl.BlockSpec(w['w_q_up'].shape, const),
            pl.BlockSpec((1, KV_LORA), const),
            pl.BlockSpec((tm, LANES), lambda i: (i % per_seq, 0)),
            pl.BlockSpec((tm, LANES), lambda i: (i % per_seq, 0)),
        ],
        out_specs=[
            pl.BlockSpec((tm, N_HEADS * HEAD_PAD), row),
            pl.BlockSpec((tm, KV_LORA), row),
            pl.BlockSpec((tm, QK_ROPE), row),
            pl.BlockSpec((tm, LANES), row),
        ],
        out_shape=[
            jax.ShapeDtypeStruct((n, N_HEADS * HEAD_PAD), BF16),
            jax.ShapeDtypeStruct((n, KV_LORA), F32),
            jax.ShapeDtypeStruct((n, QK_ROPE), F32),
            jax.ShapeDtypeStruct((n, LANES), F32),
        ],
        compiler_params=_params("parallel"),
        name="mla_proj",
    )(x, w['w_in'], w['g_q'], w['w_q_up'], w['g_kv'], cos_t, sin_t)


def _kv_up_kernel(lat_ref, krp_ref, wk_ref, wv_ref, k_ref, v_ref, *, transpose_v):
    lat = lat_ref[...].astype(BF16)
    kn = _dot(lat, wk_ref[...])
    krp = krp_ref[...].astype(BF16)
    for hd in range(N_HEADS):
        lo = hd * HEAD_PAD
        k_ref[:, lo:lo + LANES] = kn[:, hd * LANES:(hd + 1) * LANES].astype(BF16)
        k_ref[:, lo + LANES:lo + HEAD_PAD] = krp
    if transpose_v:
        v_ref[...] = lax.dot_general(wv_ref[...], lat, (((1,), (1,)), ((), ())),
                                     preferred_element_type=F32).astype(BF16)
    else:
        v_ref[...] = _dot(lat, wv_ref[...]).astype(BF16)


def _kv_up(lat, krp, w, tt, transpose_v):
    b, t, _ = lat.shape
    wv = w['w_v_t'] if transpose_v else w['w_v']
    if transpose_v:
        v_spec = pl.BlockSpec((None, N_HEADS * V_HEAD, tt), lambda bi, i: (bi, 0, i))
        v_shape = jax.ShapeDtypeStruct((b, N_HEADS * V_HEAD, t), BF16)
    else:
        v_spec = pl.BlockSpec((None, tt, N_HEADS * V_HEAD), lambda bi, i: (bi, i, 0))
        v_shape = jax.ShapeDtypeStruct((b, t, N_HEADS * V_HEAD), BF16)
    return pl.pallas_call(
        functools.partial(_kv_up_kernel, transpose_v=transpose_v),
        grid=(b, t // tt),
        in_specs=[
            pl.BlockSpec((None, tt, KV_LORA), lambda bi, i: (bi, i, 0)),
            pl.BlockSpec((None, tt, LANES), lambda bi, i: (bi, i, 0)),
            pl.BlockSpec(w['w_k'].shape, lambda bi, i: (0, 0)),
            pl.BlockSpec(wv.shape, lambda bi, i: (0, 0)),
        ],
        out_specs=[pl.BlockSpec((None, tt, N_HEADS * HEAD_PAD), lambda bi, i: (bi, i, 0)), v_spec],
        out_shape=[jax.ShapeDtypeStruct((b, t, N_HEADS * HEAD_PAD), BF16), v_shape],
        compiler_params=_params("parallel", "parallel"),
        name="kv_up",
    )(lat, krp, w['w_k'], wv)


def _scores(a, b):
    return lax.dot_general(a, b, (((1,), (1,)), ((), ())), preferred_element_type=F32)


def _tree(op, parts):
    while len(parts) > 1:
        parts = [op(parts[i], parts[i + 1]) if i + 1 < len(parts) else parts[i] for i in range(0, len(parts), 2)]
    return parts[0]


def _attn_prefill_kernel(q_ref, k_ref, vt_ref, o_ref):
    t = ATT_TILE
    k_chunk = lax.broadcasted_iota(jnp.int32, (t, t), 0) // CHUNK
    q_chunk = lax.broadcasted_iota(jnp.int32, (t, t), 1) // CHUNK
    diag_visible = k_chunk <= q_chunk
    for qi in range(q_ref.shape[0] // t):
        lo, hi = qi * t, (qi + 1) * t
        s = _scores(k_ref[0:hi, :], q_ref[lo:hi, :])
        s = [s[r:r + ATT_ROWS] for r in range(0, lo, ATT_ROWS)] + [
            jnp.where(diag_visible[r:r + ATT_ROWS], s[lo + r:lo + r + ATT_ROWS], -jnp.inf)
            for r in range(0, t, ATT_ROWS)]
        m = jnp.max(_tree(jnp.maximum, s), axis=0, keepdims=True)
        p = [jnp.exp2(blk - m) for blk in s]
        l = jnp.sum(_tree(jnp.add, p), axis=0, keepdims=True)
        p = jnp.concatenate([blk.astype(BF16) for blk in p], axis=0)
        acc = _dot(vt_ref[:, 0:hi], p)
        o_ref[lo:hi, :] = (acc / l).T.astype(o_ref.dtype)


def _attn_prefill(q, k, vt):
    b, s, _ = q.shape
    return pl.pallas_call(
        _attn_prefill_kernel,
        grid=(b, N_HEADS),
        in_specs=[
            pl.BlockSpec((None, s, HEAD_PAD), lambda bi, h: (bi, 0, h)),
            pl.BlockSpec((None, s, HEAD_PAD), lambda bi, h: (bi, 0, h)),
            pl.BlockSpec((None, V_HEAD, s), lambda bi, h: (bi, h, 0)),
        ],
        out_specs=pl.BlockSpec((None, s, V_HEAD), lambda bi, h: (bi, 0, h)),
        out_shape=jax.ShapeDtypeStruct((b, s, N_HEADS * V_HEAD), BF16),
        compiler_params=_params("parallel", "parallel"),
        name="attn_prefill",
    )(q, k, vt)


def _attn_decode_kernel(q_ref, k_ref, v_ref, o_ref, *, q_start, t_valid):
    s = _scores(q_ref[...], k_ref[...])
    q_chunk = (q_start + lax.broadcasted_iota(jnp.int32, s.shape, 0)) // CHUNK
    k_pos = lax.broadcasted_iota(jnp.int32, s.shape, 1)
    s = jnp.where((k_pos // CHUNK <= q_chunk) & (k_pos < t_valid), s, -jnp.inf)
    p = jnp.exp2(s - jnp.max(s, axis=1, keepdims=True))
    l = jnp.sum(p, axis=1, keepdims=True)
    o_ref[...] = (_dot(p.astype(BF16), v_ref[...]) / l).astype(o_ref.dtype)


def _attn_decode(q, k, v, q_start, t_valid):
    b, s, _ = q.shape
    t = k.shape[1]
    return pl.pallas_call(
        functools.partial(_attn_decode_kernel, q_start=q_start, t_valid=t_valid),
        grid=(b, N_HEADS),
        in_specs=[
            pl.BlockSpec((None, s, HEAD_PAD), lambda bi, h: (bi, 0, h)),
            pl.BlockSpec((None, t, HEAD_PAD), lambda bi, h: (bi, 0, h)),
            pl.BlockSpec((None, t, V_HEAD), lambda bi, h: (bi, 0, h)),
        ],
        out_specs=pl.BlockSpec((None, s, V_HEAD), lambda bi, h: (bi, 0, h)),
        out_shape=jax.ShapeDtypeStruct((b, s, N_HEADS * V_HEAD), BF16),
        compiler_params=_params("parallel", "parallel"),
        name="attn_decode",
    )(q, k, v)


def _proj_ln_kernel(a_ref, x_ref, w_ref, g_ref, b_ref, o_ref):
    z = ALPHA * x_ref[...] + _dot(a_ref[...], w_ref[...])
    o_ref[...] = _layer_norm(z, g_ref[...], b_ref[...])


def _proj_ln(a, x, w, g, b, tm):
    n = x.shape[0]
    row = lambda i: (i, 0)
    const = lambda i: (0, 0)
    return pl.pallas_call(
        _proj_ln_kernel,
        grid=(n // tm,),
        in_specs=[
            pl.BlockSpec((tm, a.shape[1]), row),
            pl.BlockSpec((tm, D_MODEL), row),
            pl.BlockSpec(w.shape, const),
            pl.BlockSpec((1, D_MODEL), const),
            pl.BlockSpec((1, D_MODEL), const),
        ],
        out_specs=pl.BlockSpec((tm, D_MODEL), row),
        out_shape=jax.ShapeDtypeStruct((n, D_MODEL), F32),
        compiler_params=_params("parallel"),
        name="proj_ln",
    )(a, x, w, g, b)


def _router_kernel(x_ref, wh_ref, wl_ref, comb_ref, code_ref, cnt_ref):
    x = x_ref[...]
    xh = x.astype(BF16)
    xl = (x - xh.astype(F32)).astype(BF16)
    wh = wh_ref[...]
    logits = _dot(xh, wh) + (_dot(xl, wh) + _dot(xh, wl_ref[...]))
    tm = logits.shape[0]
    lane = lax.broadcasted_iota(jnp.int32, logits.shape, 1)
    logits = jnp.where(lane < N_EXPERTS, logits, -jnp.inf)
    m1 = jnp.max(logits, axis=1, keepdims=True)
    i1 = jnp.min(jnp.where(logits == m1, lane, LANES), axis=1, keepdims=True)
    rest = jnp.where(lane == i1, -jnp.inf, logits)
    m2 = jnp.max(rest, axis=1, keepdims=True)
    i2 = jnp.min(jnp.where(rest == m2, lane, LANES), axis=1, keepdims=True)
    e2 = jnp.exp(m2 - m1)
    w1 = 1.0 / (1.0 + e2)
    comb_ref[...] = jnp.where(lane == i1, w1, jnp.where(lane == i2, e2 * w1, 0.0))
    chosen = (lane == i1) | (lane == i2)
    onehot = jnp.where(chosen, 1.0, 0.0)
    earlier = lax.broadcasted_iota(jnp.int32, (tm, tm), 0) > lax.broadcasted_iota(jnp.int32, (tm, tm), 1)
    rank = _dot(jnp.where(earlier, 1.0, 0.0).astype(BF16), onehot.astype(BF16))
    code_ref[...] = jnp.where(chosen, rank, -1.0)
    cnt_ref[...] = jnp.broadcast_to(jnp.sum(onehot, axis=0, keepdims=True), cnt_ref.shape)


def _router(x, w_hi, w_lo, tm):
    n = x.shape[0]
    row = lambda i: (i, 0)
    return pl.pallas_call(
        _router_kernel,
        grid=(n // tm,),
        in_specs=[
            pl.BlockSpec((tm, D_MODEL), row),
            pl.BlockSpec(w_hi.shape, lambda i: (0, 0)),
            pl.BlockSpec(w_lo.shape, lambda i: (0, 0)),
        ],
        out_specs=[
            pl.BlockSpec((tm, LANES), row),
            pl.BlockSpec((tm, LANES), row),
            pl.BlockSpec((8, LANES), row),
        ],
        out_shape=[
            jax.ShapeDtypeStruct((n, LANES), F32),
            jax.ShapeDtypeStruct((n, LANES), F32),
            jax.ShapeDtypeStruct((n // tm * 8, LANES), F32),
        ],
        compiler_params=_params("parallel"),
        name="router",
    )(x, w_hi, w_lo)


def _gmlp_kernel(x_ref, win_ref, g_ref, b_ref, mix_ref, bias_ref, wout_ref, ln_g_ref, ln_b_ref,
                 o_ref, v_ref, *, tm, v_rows, v_every):
    x = x_ref[...]
    h = jax.nn.gelu(_dot(x.astype(BF16), win_ref[...]))
    u = h[:, :D_MODEL]
    v = _layer_norm(h[:, D_MODEL:], g_ref[...], b_ref[...])
    vb = v.astype(BF16)
    bias = bias_ref[...]
    gated = []
    for r in range(tm // GM_MIX):
        rows = slice(r * GM_MIX, (r + 1) * GM_MIX)
        cols = []
        for grp in range(GM_GROUPS):
            lanes = slice(grp * GM_GROUP_DIM, (grp + 1) * GM_GROUP_DIM)
            cols.append(_dot(mix_ref[grp], vb[rows, lanes]))
        mixed = jnp.concatenate(cols, axis=1) + bias
        gated.append((u[rows] * mixed).astype(BF16))
    y = _dot(jnp.concatenate(gated, axis=0), wout_ref[...])
    o_ref[...] = _layer_norm(ALPHA * x + y, ln_g_ref[...], ln_b_ref[...])

    @pl.when(pl.program_id(0) % v_every == v_every - 1)
    def _():
        v_ref[...] = v[tm - v_rows:, :]


def _gmlp(x, w, mix, bias, ln_g, ln_b, tm, v_rows, v_every):
    n = x.shape[0]
    row = lambda i: (i, 0)
    const = lambda i: (0, 0)
    n_v = n // (tm * v_every) * v_rows
    return pl.pallas_call(
        functools.partial(_gmlp_kernel, tm=tm, v_rows=v_rows, v_every=v_every),
        grid=(n // tm,),
        in_specs=[
            pl.BlockSpec((tm, D_MODEL), row),
            pl.BlockSpec(w['w_in'].shape, const),
            pl.BlockSpec((1, D_MODEL), const),
            pl.BlockSpec((1, D_MODEL), const),
            pl.BlockSpec(mix.shape, lambda i: (0, 0, 0)),
            pl.BlockSpec(bias.shape, const),
            pl.BlockSpec(w['w_out'].shape, const),
            pl.BlockSpec((1, D_MODEL), const),
            pl.BlockSpec((1, D_MODEL), const),
        ],
        out_specs=[
            pl.BlockSpec((tm, D_MODEL), row),
            pl.BlockSpec((v_rows, D_MODEL), lambda i: (i // v_every, 0)),
        ],
        out_shape=[
            jax.ShapeDtypeStruct((n, D_MODEL), F32),
            jax.ShapeDtypeStruct((n_v, D_MODEL), F32),
        ],
        compiler_params=_params("arbitrary"),
        name="gmlp",
    )(x, w['w_in'], w['g_norm'], w['b_norm'], mix, bias, w['w_out'], ln_g, ln_b)


def _ln_ple(z, g_ref, b_ref, p_ref, wpg_ref, wpp_ref):
    x2 = _layer_norm(z, g_ref[...], b_ref[...])
    gate = jax.nn.sigmoid(_dot(x2.astype(BF16), wpg_ref[...]))
    return x2 + gate * _dot(p_ref[...].astype(BF16), wpp_ref[...])


def _swiglu(xb, wg_ref, wu_ref, wd_ref):
    d_ff = wg_ref.shape[1]
    y = None
    for lo in range(0, d_ff, FF_CHUNK):
        hi = min(lo + FF_CHUNK, d_ff)
        h = jax.nn.silu(_dot(xb, wg_ref[:, lo:hi])) * _dot(xb, wu_ref[:, lo:hi])
        d = _dot(h.astype(BF16), wd_ref[lo:hi, :])
        y = d if y is None else y + d
    return y


def _ffn_kernel(x_ref, wg_ref, wu_ref, wd_ref, g_ref, b_ref, p_ref, wpg_ref, wpp_ref, o_ref):
    x = x_ref[...]
    y = _swiglu(x.astype(BF16), wg_ref, wu_ref, wd_ref)
    o_ref[...] = _ln_ple(ALPHA * x + y, g_ref, b_ref, p_ref, wpg_ref, wpp_ref)


def _resident(shape):
    return pl.BlockSpec(shape, lambda i: (0,) * len(shape), pipeline_mode=pl.Buffered(1))


def _ffn(x, wg, wu, wd, ln_g, ln_b, p, w_ple_gate, w_ple_proj, tm):
    n = x.shape[0]
    row = lambda i: (i, 0)
    const = lambda i: (0, 0)
    return pl.pallas_call(
        _ffn_kernel,
        grid=(n // tm,),
        in_specs=[
            pl.BlockSpec((tm, D_MODEL), row),
            _resident(wg.shape),
            _resident(wu.shape),
            _resident(wd.shape),
            pl.BlockSpec((1, D_MODEL), const),
            pl.BlockSpec((1, D_MODEL), const),
            pl.BlockSpec((tm, p.shape[1]), row),
            _resident(w_ple_gate.shape),
            _resident(w_ple_proj.shape),
        ],
        out_specs=pl.BlockSpec((tm, D_MODEL), row),
        out_shape=jax.ShapeDtypeStruct((n, D_MODEL), F32),
        compiler_params=_params("parallel"),
        name="ffn",
    )(x, wg, wu, wd, ln_g, ln_b, p, w_ple_gate, w_ple_proj)


def _moe_plan(cnt, n_tiles, tm_ffn, n_ffn_tiles):
    cnt = cnt.reshape(n_tiles, 8, LANES)[:, 0, :N_EXPERTS].astype(jnp.int32)
    nch = (cnt + MOE_CHUNK - 1) // MOE_CHUNK
    base = jnp.cumsum(nch, axis=1) - nch
    seg = (cnt + BF16_ROWS - 1) // BF16_ROWS * BF16_ROWS
    group_rows = jnp.sum(seg, axis=0) + MOE_CHUNK
    group_pad = (group_rows + tm_ffn - 1) // tm_ffn * tm_ffn
    group_start = jnp.cumsum(group_pad) - group_pad
    seg_start = group_start[None, :] + jnp.cumsum(seg, axis=0) - seg
    tile_end = jnp.cumsum(group_pad // tm_ffn)
    n_used = tile_end[-1]
    tile_id = jnp.minimum(jnp.arange(n_ffn_tiles), n_used - 1)
    tile_expert = jnp.sum(tile_id[:, None] >= tile_end[None, :], axis=1)
    base_rows = jnp.zeros((n_tiles, 8, LANES), F32).at[:, :, :N_EXPERTS].set(
        (base * MOE_CHUNK).astype(F32)[:, None, :]).reshape(n_tiles * 8, LANES)
    flat = lambda a: a.reshape(-1).astype(jnp.int32)
    return dict(seg=flat(seg_start), nch=flat(nch), base=flat(base), base_rows=base_rows,
                tile_expert=flat(tile_expert), n_used=flat(n_used))


def _staged_rows(code, base_rows):
    pos = jnp.where(code >= 0.0, base_rows + code, -1.0)
    first = jnp.max(pos, axis=1, keepdims=True)
    second = jnp.max(jnp.where(pos == first, -1.0, pos), axis=1, keepdims=True)
    return pos, first, second


def _chunk_copies(hbm_ref, vmem_ref, sem, seg_ref, nch_ref, base_ref, tile, slot, to_hbm, action):
    for e in range(N_EXPERTS):
        idx = tile * N_EXPERTS + e
        for c in range(TOKEN_TILE // MOE_CHUNK):
            @pl.when(c < nch_ref[idx])
            def _():
                stage = vmem_ref.at[slot, pl.ds(pl.multiple_of((base_ref[idx] + c) * MOE_CHUNK, MOE_CHUNK), MOE_CHUNK)]
                group = hbm_ref.at[pl.ds(pl.multiple_of(seg_ref[idx] + c * MOE_CHUNK, BF16_ROWS), MOE_CHUNK)]
                src, dst = (stage, group) if to_hbm else (group, stage)
                copy = pltpu.make_async_copy(src, dst, sem.at[slot])
                copy.start() if action == "start" else copy.wait()


def _dispatch_kernel(seg_ref, nch_ref, base_ref, x_ref, code_ref, brow_ref, zeros_ref, xs_ref, stage_ref, sem):
    del zeros_ref
    t = pl.program_id(0)
    slot = t % 2
    _, first, second = _staged_rows(code_ref[...], brow_ref[0:1, :])
    lane = lax.broadcasted_iota(jnp.int32, (TOKEN_TILE, LANES), 1)
    rows_t = jnp.where(lane == 0, first, jnp.where(lane == 1, second, -1.0)).T
    r1, r2 = rows_t[0:1, :], rows_t[1:2, :]
    xb = x_ref[...].astype(BF16)
    n_chunks = base_ref[t * N_EXPERTS + N_EXPERTS - 1] + nch_ref[t * N_EXPERTS + N_EXPERTS - 1]
    for g in range(MOE_STAGE // MOE_GROUP):
        @pl.when(g * (MOE_GROUP // MOE_CHUNK) < n_chunks)
        def _():
            row = (lax.broadcasted_iota(jnp.int32, (MOE_GROUP, TOKEN_TILE), 0) + g * MOE_GROUP).astype(F32)
            sel = jnp.where((row == r1) | (row == r2), 1.0, 0.0).astype(BF16)
            stage_ref[slot, g * MOE_GROUP:(g + 1) * MOE_GROUP, :] = _dot(sel, xb).astype(BF16)

    copies = functools.partial(_chunk_copies, xs_ref, stage_ref, sem, seg_ref, nch_ref, base_ref, to_hbm=True)

    @pl.when(t > 0)
    def _():
        copies(tile=t - 1, slot=1 - slot, action="wait")

    copies(tile=t, slot=slot, action="start")

    @pl.when(t == pl.num_programs(0) - 1)
    def _():
        copies(tile=t, slot=slot, action="wait")


def _dispatch(x, code, plan, n_rows):
    n = x.shape[0]
    row = lambda i, *_: (i, 0)
    grid_spec = pltpu.PrefetchScalarGridSpec(
        num_scalar_prefetch=3,
        grid=(n // TOKEN_TILE,),
        in_specs=[
            pl.BlockSpec((TOKEN_TILE, D_MODEL), row),
            pl.BlockSpec((TOKEN_TILE, LANES), row),
            pl.BlockSpec((8, LANES), row),
            pl.BlockSpec(memory_space=pl.ANY),
        ],
        out_specs=pl.BlockSpec(memory_space=pl.ANY),
        scratch_shapes=[pltpu.VMEM((2, MOE_STAGE, D_MODEL), BF16), pltpu.SemaphoreType.DMA((2,))],
    )
    return pl.pallas_call(
        _dispatch_kernel,
        grid_spec=grid_spec,
        out_shape=jax.ShapeDtypeStruct((n_rows, D_MODEL), BF16),
        input_output_aliases={6: 0},
        compiler_params=_params("arbitrary"),
        name="moe_dispatch",
    )(plan['seg'], plan['nch'], plan['base'], x, code, plan['base_rows'], jnp.zeros((n_rows, D_MODEL), BF16))


def _moe_ffn_kernel(te_ref, nu_ref, x_ref, wg_ref, wu_ref, wd_ref, o_ref, acc_ref):
    del te_ref
    f = pl.program_id(1)
    used = pl.program_id(0) < nu_ref[0]

    @pl.when(jnp.logical_not(used) & (f == 0))
    def _():
        o_ref[...] = jnp.zeros(o_ref.shape, o_ref.dtype)

    @pl.when(used)
    def _():
        @pl.when(f == 0)
        def _():
            acc_ref[...] = jnp.zeros(acc_ref.shape, F32)

        acc_ref[...] += _swiglu(x_ref[...], wg_ref, wu_ref, wd_ref)

        @pl.when(f == pl.num_programs(1) - 1)
        def _():
            o_ref[...] = acc_ref[...].astype(o_ref.dtype)


def _moe_ffn(xs, plan, wg, wu, wd, tm, tf):
    n_rows = xs.shape[0]
    n_f = wg.shape[-1] // tf
    tile = lambda i, f, te, nu: (jnp.minimum(i, nu[0] - 1), 0)
    f_idx = lambda i, f, nu: jnp.where(i < nu[0], f, n_f - 1)
    grid_spec = pltpu.PrefetchScalarGridSpec(
        num_scalar_prefetch=2,
        grid=(n_rows // tm, n_f),
        in_specs=[
            pl.BlockSpec((tm, D_MODEL), tile),
            pl.BlockSpec((None, D_MODEL, tf), lambda i, f, te, nu: (te[i], 0, f_idx(i, f, nu))),
            pl.BlockSpec((None, D_MODEL, tf), lambda i, f, te, nu: (te[i], 0, f_idx(i, f, nu))),
            pl.BlockSpec((None, tf, D_MODEL), lambda i, f, te, nu: (te[i], f_idx(i, f, nu), 0)),
        ],
        out_specs=pl.BlockSpec((tm, D_MODEL), lambda i, f, te, nu: (i, 0)),
        scratch_shapes=[pltpu.VMEM((tm, D_MODEL), F32)],
    )
    return pl.pallas_call(
        _moe_ffn_kernel,
        grid_spec=grid_spec,
        out_shape=jax.ShapeDtypeStruct((n_rows, D_MODEL), BF16),
        compiler_params=_params("arbitrary", "arbitrary"),
        name="moe_ffn",
    )(plan['tile_expert'], plan['n_used'], xs, wg, wu, wd)


def _combine_kernel(seg_ref, nch_ref, base_ref, x_ref, comb_ref, code_ref, brow_ref, ys_ref,
                    g_ref, b_ref, p_ref, wpg_ref, wpp_ref, o_ref, buf_ref, y_ref, sem):
    t = pl.program_id(0)
    slot = t % 2
    copies = functools.partial(_chunk_copies, ys_ref, buf_ref, sem, seg_ref, nch_ref, base_ref, to_hbm=False)

    @pl.when(t == 0)
    def _():
        buf_ref[...] = jnp.zeros(buf_ref.shape, BF16)
        copies(tile=t, slot=slot, action="start")

    @pl.when(t + 1 < pl.num_programs(0))
    def _():
        copies(tile=t + 1, slot=1 - slot, action="start")

    copies(tile=t, slot=slot, action="wait")

    comb = comb_ref[...]
    pos, first, second = _staged_rows(code_ref[...], brow_ref[0:1, :])
    w_first = jnp.max(jnp.where(pos == first, comb, 0.0), axis=1, keepdims=True)
    w_second = jnp.max(jnp.where(pos == second, comb, 0.0), axis=1, keepdims=True)
    y_ref[...] = jnp.zeros(y_ref.shape, F32)
    n_chunks = base_ref[t * N_EXPERTS + N_EXPERTS - 1] + nch_ref[t * N_EXPERTS + N_EXPERTS - 1]
    for g in range(MOE_STAGE // MOE_GROUP):
        @pl.when(g * (MOE_GROUP // MOE_CHUNK) < n_chunks)
        def _():
            row = (lax.broadcasted_iota(jnp.int32, (TOKEN_TILE, MOE_GROUP), 1) + g * MOE_GROUP).astype(F32)
            rows = buf_ref[slot, g * MOE_GROUP:(g + 1) * MOE_GROUP, :]
            y_ref[...] += (w_first * _dot(jnp.where(row == first, 1.0, 0.0).astype(BF16), rows)
                           + w_second * _dot(jnp.where(row == second, 1.0, 0.0).astype(BF16), rows))

    o_ref[...] = _ln_ple(ALPHA * x_ref[...] + y_ref[...], g_ref, b_ref, p_ref, wpg_ref, wpp_ref)


def _combine(x, comb, code, plan, ys, ln_g, ln_b, p, w_ple_gate, w_ple_proj):
    n = x.shape[0]
    row = lambda i, *_: (i, 0)
    const = lambda i, *_: (0, 0)
    grid_spec = pltpu.PrefetchScalarGridSpec(
        num_scalar_prefetch=3,
        grid=(n // TOKEN_TILE,),
        in_specs=[
            pl.BlockSpec((TOKEN_TILE, D_MODEL), row),
            pl.BlockSpec((TOKEN_TILE, LANES), row),
            pl.BlockSpec((TOKEN_TILE, LANES), row),
            pl.BlockSpec((8, LANES), row),
            pl.BlockSpec(memory_space=pl.ANY),
            pl.BlockSpec((1, D_MODEL), const),
            pl.BlockSpec((1, D_MODEL), const),
            pl.BlockSpec((TOKEN_TILE, p.shape[1]), row),
            pl.BlockSpec(w_ple_gate.shape, const),
            pl.BlockSpec(w_ple_proj.shape, const),
        ],
        out_specs=pl.BlockSpec((TOKEN_TILE, D_MODEL), row),
        scratch_shapes=[pltpu.VMEM((2, MOE_STAGE, D_MODEL), BF16), pltpu.VMEM((TOKEN_TILE, D_MODEL), F32),
                        pltpu.SemaphoreType.DMA((2,))],
    )
    return pl.pallas_call(
        _combine_kernel,
        grid_spec=grid_spec,
        out_shape=jax.ShapeDtypeStruct((n, D_MODEL), F32),
        compiler_params=_params("arbitrary"),
        name="moe_combine",
    )(plan['seg'], plan['nch'], plan['base'], x, comb, code, plan['base_rows'], ys,
      ln_g, ln_b, p, w_ple_gate, w_ple_proj)


def _moe(x, comb, code, cnt, wg, wu, wd, ln_g, ln_b, p, w_ple_gate, w_ple_proj):
    n = x.shape[0]
    n_tiles = n // TOKEN_TILE
    tm_ffn = 1024 if n >= 8192 else 256
    max_rows = TOP_K * n + n_tiles * N_EXPERTS * (BF16_ROWS - 1) + N_EXPERTS * (MOE_CHUNK + tm_ffn - 1)
    n_ffn_tiles = -(-max_rows // tm_ffn)
    plan = _moe_plan(cnt, n_tiles, tm_ffn, n_ffn_tiles)
    xs = _dispatch(x, code, plan, n_ffn_tiles * tm_ffn)
    ys = _moe_ffn(xs, plan, wg, wu, wd, tm_ffn, MOE_FF_BLOCK)
    return _combine(x, comb, code, plan, ys, ln_g, ln_b, p, w_ple_gate, w_ple_proj)


def _rope_tables(pos):
    inv = ROPE_THETA ** (-jnp.arange(ROPE_HALF, dtype=F32) / ROPE_HALF)
    ang = pos.astype(F32)[:, None] * inv[None, :]
    cos, sin, zero = jnp.cos(ang), jnp.sin(ang), jnp.zeros_like(ang)
    return (jnp.concatenate([cos, cos, zero, zero], axis=1),
            jnp.concatenate([-sin, sin, zero, zero], axis=1))


def _prep_mla(w_in, g_q, w_q_up, g_kv, w_kv_up, w_out):
    k_r = w_in[:, Q_LORA + KV_LORA:]
    w_in_p = jnp.concatenate([w_in[:, :Q_LORA + KV_LORA], k_r, k_r], axis=1)
    wq = w_q_up.reshape(Q_LORA, N_HEADS, QK_NOPE + QK_ROPE)
    wq_p = jnp.concatenate([wq, wq[:, :, QK_NOPE:]], axis=2).reshape(Q_LORA, N_HEADS * HEAD_PAD)
    wkv = w_kv_up.reshape(KV_LORA, N_HEADS, QK_NOPE + V_HEAD)
    w_k = wkv[:, :, :QK_NOPE].reshape(KV_LORA, -1).astype(BF16)
    w_v = wkv[:, :, QK_NOPE:].reshape(KV_LORA, -1).astype(BF16)
    return dict(w_in=w_in_p.astype(BF16), g_q=g_q[None, :], w_q_up=wq_p.astype(BF16), g_kv=g_kv[None, :],
                w_k=w_k, w_v=w_v, w_v_t=w_v.T, w_out=w_out.astype(BF16))


def _prep_gmlp_mix(w_s, b_s, chunk):
    wm = jnp.tril(w_s[:, :chunk, :chunk])
    reps = GM_MIX // chunk
    mix = jnp.einsum('ab,gts->gatbs', jnp.eye(reps, dtype=F32), wm).reshape(GM_GROUPS, GM_MIX, GM_MIX)
    bias = jnp.repeat(jnp.tile(b_s[:, :chunk].T, (reps, 1)), GM_GROUP_DIM, axis=1)
    return mix.astype(BF16), bias


def _row(v):
    return v[None, :]


def _trunk(x, p, past_latent, past_krope, W):
    b, s, _ = x.shape
    n = b * s
    tm = TOKEN_TILE
    past = 0 if past_latent is None else past_latent.shape[2]
    cos_t, sin_t = _rope_tables(past + jnp.arange(s))
    if s < tm:
        cos_t, sin_t = jnp.tile(cos_t, (tm // s, 1)), jnp.tile(sin_t, (tm // s, 1))
    x = x.reshape(n, D_MODEL)
    chunk = min(s, GM_CHUNK)
    new_lat, new_kr, new_v = [], [], []
    for i in range(DEPTH):
        j = i // 2
        ln1 = (_row(W['ln1_g'][i]), _row(W['ln1_b'][i]))
        tail = (_row(W['ln2_g'][i]), _row(W['ln2_b'][i]), p[i].reshape(n, -1), W['w_ple_gate'][i], W['w_ple_proj'][i])
        if i % 2 == 0:
            w = W['mla'][j]
            q, lat, kr, krp = _mla_proj(x, w, cos_t, sin_t, tm)
            lat = lat.reshape(b, s, KV_LORA)
            krp = krp.reshape(b, s, LANES)
            new_lat.append(lat)
            new_kr.append(kr.reshape(b, s, QK_ROPE))
            q = q.reshape(b, s, -1)
            if past_latent is None:
                k, vt = _kv_up(lat, krp, w, tm, True)
                att = _attn_prefill(q, k, vt)
            else:
                t_valid = past + s
                t_pad = -(-t_valid // LANES) * LANES
                lat_all = jnp.concatenate(
                    [past_latent[j], lat, jnp.zeros((b, t_pad - t_valid, KV_LORA), F32)], axis=1)
                krp_all = jnp.concatenate(
                    [jnp.pad(past_krope[j], ((0, 0), (0, 0), (0, LANES - QK_ROPE))), krp,
                     jnp.zeros((b, t_pad - t_valid, LANES), F32)], axis=1)
                k, v = _kv_up(lat_all, krp_all, w, t_pad, False)
                att = _attn_decode(q, k, v, past, t_valid)
            x = _proj_ln(att.reshape(n, -1), x, w['w_out'], *ln1, tm)
            x = _ffn(x, W['w_ffn_gate'][j], W['w_ffn_up'][j], W['w_ffn_down'][j], *tail, tm)
        else:
            w = W['gm'][j]
            mix, bias = _prep_gmlp_mix(w['w_spatial'], w['b_spatial'], chunk)
            v_rows, v_every = (GM_CHUNK, s // tm) if s >= GM_CHUNK else (tm, 1)
            x, v_new = _gmlp(x, w, mix, bias, *ln1, tm, v_rows, v_every)
            new_v.append(v_new.reshape(b, chunk, D_MODEL))
            comb, code, cnt = _router(x, W['w_router_hi'][j], W['w_router_lo'][j], tm)
            x = _moe(x, comb, code, cnt, W['w_moe_gate'][j], W['w_moe_up'][j], W['w_moe_down'][j], *tail)
    return x.reshape(b, s, D_MODEL), jnp.stack(new_lat), jnp.stack(new_kr), jnp.stack(new_v)


def kernel(x_prompt, x_sample, cache_mla_latent, cache_mla_krope, p_prompt, p_sample, w_mla_in, g_q_norm, w_q_up, g_kv_norm, w_kv_up, w_mla_out, w_gm_in, g_gm_norm, b_gm_norm, w_gm_spatial, b_gm_spatial, w_gm_out, w_ffn_gate, w_ffn_up, w_ffn_down, w_router, w_moe_gate, w_moe_up, w_moe_down, ln1_g, ln1_b, ln2_g, ln2_b, w_ple_gate, w_ple_proj):
    n_mla, n_gm = w_mla_in.shape[0], w_gm_in.shape[0]
    w_router = jnp.pad(w_router, ((0, 0), (0, 0), (0, LANES - N_EXPERTS)))
    w_router_hi = w_router.astype(BF16)
    W = dict(
        mla=[_prep_mla(w_mla_in[j], g_q_norm[j], w_q_up[j], g_kv_norm[j], w_kv_up[j], w_mla_out[j])
             for j in range(n_mla)],
        gm=[dict(w_in=w_gm_in[j].astype(BF16), g_norm=_row(g_gm_norm[j]), b_norm=_row(b_gm_norm[j]),
                 w_spatial=w_gm_spatial[j], b_spatial=b_gm_spatial[j], w_out=w_gm_out[j].astype(BF16))
            for j in range(n_gm)],
        w_ffn_gate=w_ffn_gate.astype(BF16), w_ffn_up=w_ffn_up.astype(BF16), w_ffn_down=w_ffn_down.astype(BF16),
        w_router_hi=w_router_hi, w_router_lo=(w_router - w_router_hi.astype(F32)).astype(BF16),
        w_moe_gate=w_moe_gate.astype(BF16), w_moe_up=w_moe_up.astype(BF16), w_moe_down=w_moe_down.astype(BF16),
        ln1_g=ln1_g, ln1_b=ln1_b, ln2_g=ln2_g, ln2_b=ln2_b,
        w_ple_gate=w_ple_gate.astype(BF16), w_ple_proj=w_ple_proj.astype(BF16),
    )
    y_p, lat_p, kr_p, v_p = _trunk(x_prompt, p_prompt, None, None, W)
    y_s, lat_s, kr_s, v_s = _trunk(x_sample, p_sample, cache_mla_latent, cache_mla_krope, W)
    return (y_p, y_s, lat_p, kr_p, lat_s, kr_s, v_p, v_s)
```

```python
import functools

import jax
import jax.numpy as jnp
from jax import lax
from jax.experimental import pallas as pl
from jax.experimental.pallas import tpu as pltpu

F32 = jnp.float32
BF16 = jnp.bfloat16

D_MODEL = 1024
DEPTH = 4
CHUNK = 64
N_HEADS = 8
QK_NOPE = 128
QK_ROPE = 64
V_HEAD = 128
Q_LORA = 384
KV_LORA = 256
ROPE_THETA = 10000.0
ATTN_SCALE = (QK_NOPE + QK_ROPE) ** -0.5
GM_CHUNK = 128
GM_GROUPS = 8
GM_GROUP_DIM = D_MODEL // GM_GROUPS
N_EXPERTS = 8
TOP_K = 2
ALPHA = (2 * DEPTH) ** 0.25

LANES = 128
BF16_ROWS = 16
HEAD_PAD = 2 * LANES
ROPE_HALF = QK_ROPE // 2
LOG2E = 1.4426950408889634
Q_SCALE = ATTN_SCALE * LOG2E
ATT_TILE = 256
ATT_ROWS = 32
GM_MIX = 256
FF_CHUNK = 1024
MOE_FF_BLOCK = 1792
TOKEN_TILE = 512
MOE_CHUNK = 128
MOE_SLOTS = TOP_K * TOKEN_TILE // MOE_CHUNK + N_EXPERTS
MOE_STAGE = MOE_SLOTS * MOE_CHUNK
MOE_GROUP = 512
VMEM_LIMIT = 56 * 1024 * 1024


def _params(*sem):
    return pltpu.CompilerParams(dimension_semantics=sem, vmem_limit_bytes=VMEM_LIMIT)


def _dot(a, b):
    return jnp.dot(a, b, preferred_element_type=F32)


def _layer_norm(z, g, b):
    mu = jnp.mean(z, -1, keepdims=True)
    zc = z - mu
    var = jnp.mean(zc * zc, -1, keepdims=True)
    return zc * lax.rsqrt(var + 1e-5) * g + b


def _rms_norm(z, g):
    return z * lax.rsqrt(jnp.mean(z * z, -1, keepdims=True) + 1e-6) * g


def _rope(r, cos_t, sin_t):
    return r * cos_t + pltpu.roll(r, ROPE_HALF, 1) * sin_t


def _mla_proj_kernel(x_ref, win_ref, gq_ref, wq_ref, gkv_ref, cos_ref, sin_ref,
                     q_ref, lat_ref, kr_ref, krp_ref):
    xb = x_ref[...].astype(BF16)
    h = _dot(xb, win_ref[...])
    cos_t = cos_ref[...]
    sin_t = sin_ref[...]
    cq = _rms_norm(h[:, :Q_LORA], gq_ref[...])
    q = _dot(cq.astype(BF16), wq_ref[...])
    for hd in range(N_HEADS):
        lo = hd * HEAD_PAD
        q_ref[:, lo:lo + LANES] = (q[:, lo:lo + LANES] * Q_SCALE).astype(BF16)
        rot = _rope(q[:, lo + LANES:lo + HEAD_PAD], cos_t, sin_t) * Q_SCALE
        q_ref[:, lo + LANES:lo + HEAD_PAD] = rot.astype(BF16)
    lat_ref[...] = _rms_norm(h[:, Q_LORA:Q_LORA + KV_LORA], gkv_ref[...])
    krot = _rope(h[:, Q_LORA + KV_LORA:], cos_t, sin_t)
    krp_ref[...] = krot
    kr_ref[...] = krot[:, :QK_ROPE]


def _mla_proj(x, w, cos_t, sin_t, tm):
    n = x.shape[0]
    per_seq = cos_t.shape[0] // tm
    row = lambda i: (i, 0)
    const = lambda i: (0, 0)
    return pl.pallas_call(
        _mla_proj_kernel,
        grid=(n // tm,),
        in_specs=[
            pl.BlockSpec((tm, D_MODEL), row),
            pl.BlockSpec(w['w_in'].shape, const),
            pl.BlockSpec((1, Q_LORA), const),
            pl.BlockSpec(w['w_q_up'].shape, const),
            pl.BlockSpec((1, KV_LORA), const),
            pl.BlockSpec((tm, LANES), lambda i: (i % per_seq, 0)),
            pl.BlockSpec((tm, LANES), lambda i: (i % per_seq, 0)),
        ],
        out_specs=[
            pl.BlockSpec((tm, N_HEADS * HEAD_PAD), row),
            pl.BlockSpec((tm, KV_LORA), row),
            pl.BlockSpec((tm, QK_ROPE), row),
            pl.BlockSpec((tm, LANES), row),
        ],
        out_shape=[
            jax.ShapeDtypeStruct((n, N_HEADS * HEAD_PAD), BF16),
            jax.ShapeDtypeStruct((n, KV_LORA), F32),
            jax.ShapeDtypeStruct((n, QK_ROPE), F32),
            jax.ShapeDtypeStruct((n, LANES), F32),
        ],
        compiler_params=_params("parallel"),
        name="mla_proj",
    )(x, w['w_in'], w['g_q'], w['w_q_up'], w['g_kv'], cos_t, sin_t)


def _kv_up_kernel(lat_ref, krp_ref, wk_ref, wv_ref, k_ref, v_ref, *, transpose_v):
    lat = lat_ref[...].astype(BF16)
    kn = _dot(lat, wk_ref[...])
    krp = krp_ref[...].astype(BF16)
    for hd in range(N_HEADS):
        lo = hd * HEAD_PAD
        k_ref[:, lo:lo + LANES] = kn[:, hd * LANES:(hd + 1) * LANES].astype(BF16)
        k_ref[:, lo + LANES:lo + HEAD_PAD] = krp
    if transpose_v:
        v_ref[...] = lax.dot_general(wv_ref[...], lat, (((1,), (1,)), ((), ())),
                                     preferred_element_type=F32).astype(BF16)
    else:
        v_ref[...] = _dot(lat, wv_ref[...]).astype(BF16)


def _kv_up(lat, krp, w, tt, transpose_v):
    b, t, _ = lat.shape
    wv = w['w_v_t'] if transpose_v else w['w_v']
    if transpose_v:
        v_spec = pl.BlockSpec((None, N_HEADS * V_HEAD, tt), lambda bi, i: (bi, 0, i))
        v_shape = jax.ShapeDtypeStruct((b, N_HEADS * V_HEAD, t), BF16)
    else:
        v_spec = pl.BlockSpec((None, tt, N_HEADS * V_HEAD), lambda bi, i: (bi, i, 0))
        v_shape = jax.ShapeDtypeStruct((b, t, N_HEADS * V_HEAD), BF16)
    return pl.pallas_call(
        functools.partial(_kv_up_kernel, transpose_v=transpose_v),
        grid=(b, t // tt),
        in_specs=[
            pl.BlockSpec((None, tt, KV_LORA), lambda bi, i: (bi, i, 0)),
            pl.BlockSpec((None, tt, LANES), lambda bi, i: (bi, i, 0)),
            pl.BlockSpec(w['w_k'].shape, lambda bi, i: (0, 0)),
            pl.BlockSpec(wv.shape, lambda bi, i: (0, 0)),
        ],
        out_specs=[pl.BlockSpec((None, tt, N_HEADS * HEAD_PAD), lambda bi, i: (bi, i, 0)), v_spec],
        out_shape=[jax.ShapeDtypeStruct((b, t, N_HEADS * HEAD_PAD), BF16), v_shape],
        compiler_params=_params("parallel", "parallel"),
        name="kv_up",
    )(lat, krp, w['w_k'], wv)


def _scores(a, b):
    return lax.dot_general(a, b, (((1,), (1,)), ((), ())), preferred_element_type=F32)


def _tree(op, parts):
    while len(parts) > 1:
        parts = [op(parts[i], parts[i + 1]) if i + 1 < len(parts) else parts[i] for i in range(0, len(parts), 2)]
    return parts[0]


def _attn_prefill_kernel(q_ref, k_ref, vt_ref, o_ref, s_ref, p_ref):
    t = ATT_TILE
    n_q = q_ref.shape[0] // t
    k_chunk = lax.broadcasted_iota(jnp.int32, (t, t), 0) // CHUNK
    q_chunk = lax.broadcasted_iota(jnp.int32, (t, t), 1) // CHUNK
    diag_visible = k_chunk <= q_chunk

    def scores(qi):
        lo, hi = qi * t, (qi + 1) * t
        s_ref[qi % 2, 0:hi, :] = _scores(k_ref[0:hi, :], q_ref[lo:hi, :])

    def softmax(qi):
        lo, slot = qi * t, qi % 2
        s = [s_ref[slot, r:r + ATT_ROWS, :] for r in range(0, lo, ATT_ROWS)] + [
            jnp.where(diag_visible[r:r + ATT_ROWS], s_ref[slot, lo + r:lo + r + ATT_ROWS, :], -jnp.inf)
            for r in range(0, t, ATT_ROWS)]
        m = jnp.max(_tree(jnp.maximum, s), axis=0, keepdims=True)
        p = [jnp.exp2(blk - m) for blk in s]
        for i, blk in enumerate(p):
            p_ref[slot, i * ATT_ROWS:(i + 1) * ATT_ROWS, :] = blk.astype(BF16)
        return jnp.sum(_tree(jnp.add, p), axis=0, keepdims=True)

    def values(qi, l):
        lo, hi = qi * t, (qi + 1) * t
        acc = _dot(vt_ref[:, 0:hi], p_ref[qi % 2, 0:hi, :])
        o_ref[lo:hi, :] = (acc / l).T.astype(o_ref.dtype)

    scores(0)
    for qi in range(n_q):
        if qi + 1 < n_q:
            scores(qi + 1)
        values(qi, softmax(qi))


def _attn_prefill(q, k, vt):
    b, s, _ = q.shape
    return pl.pallas_call(
        _attn_prefill_kernel,
        grid=(b, N_HEADS),
        in_specs=[
            pl.BlockSpec((None, s, HEAD_PAD), lambda bi, h: (bi, 0, h)),
            pl.BlockSpec((None, s, HEAD_PAD), lambda bi, h: (bi, 0, h)),
            pl.BlockSpec((None, V_HEAD, s), lambda bi, h: (bi, h, 0)),
        ],
        out_specs=pl.BlockSpec((None, s, V_HEAD), lambda bi, h: (bi, 0, h)),
        out_shape=jax.ShapeDtypeStruct((b, s, N_HEADS * V_HEAD), BF16),
        scratch_shapes=[pltpu.VMEM((2, s, ATT_TILE), F32), pltpu.VMEM((2, s, ATT_TILE), BF16)],
        compiler_params=_params("parallel", "parallel"),
        name="attn_prefill",
    )(q, k, vt)


def _attn_decode_kernel(q_ref, k_ref, v_ref, o_ref, *, q_start, t_valid):
    s_q, t = q_ref.shape[0], k_ref.shape[0]
    q_chunk = (q_start + lax.broadcasted_iota(jnp.int32, (s_q, t), 0)) // CHUNK
    k_pos = lax.broadcasted_iota(jnp.int32, (s_q, t), 1)
    visible = (k_pos // CHUNK <= q_chunk) & (k_pos < t_valid)
    for hd in range(N_HEADS):
        qk = slice(hd * HEAD_PAD, (hd + 1) * HEAD_PAD)
        vo = slice(hd * V_HEAD, (hd + 1) * V_HEAD)
        s = jnp.where(visible, _scores(q_ref[:, qk], k_ref[:, qk]), -jnp.inf)
        p = jnp.exp2(s - jnp.max(s, axis=1, keepdims=True))
        l = jnp.sum(p, axis=1, keepdims=True)
        o_ref[:, vo] = (_dot(p.astype(BF16), v_ref[:, vo]) / l).astype(o_ref.dtype)


def _attn_decode(q, k, v, q_start, t_valid):
    b, s, _ = q.shape
    t = k.shape[1]
    seq = lambda bi: (bi, 0, 0)
    return pl.pallas_call(
        functools.partial(_attn_decode_kernel, q_start=q_start, t_valid=t_valid),
        grid=(b,),
        in_specs=[
            pl.BlockSpec((None, s, N_HEADS * HEAD_PAD), seq),
            pl.BlockSpec((None, t, N_HEADS * HEAD_PAD), seq),
            pl.BlockSpec((None, t, N_HEADS * V_HEAD), seq),
        ],
        out_specs=pl.BlockSpec((None, s, N_HEADS * V_HEAD), seq),
        out_shape=jax.ShapeDtypeStruct((b, s, N_HEADS * V_HEAD), BF16),
        compiler_params=_params("parallel"),
        name="attn_decode",
    )(q, k, v)


def _router_kernel(x_ref, wh_ref, wl_ref, comb_ref, code_ref, cnt_ref):
    x = x_ref[...]
    xh = x.astype(BF16)
    xl = (x - xh.astype(F32)).astype(BF16)
    wh = wh_ref[...]
    logits = _dot(xh, wh) + (_dot(xl, wh) + _dot(xh, wl_ref[...]))
    tm = logits.shape[0]
    lane = lax.broadcasted_iota(jnp.int32, logits.shape, 1)
    logits = jnp.where(lane < N_EXPERTS, logits, -jnp.inf)
    m1 = jnp.max(logits, axis=1, keepdims=True)
    i1 = jnp.min(jnp.where(logits == m1, lane, LANES), axis=1, keepdims=True)
    rest = jnp.where(lane == i1, -jnp.inf, logits)
    m2 = jnp.max(rest, axis=1, keepdims=True)
    i2 = jnp.min(jnp.where(rest == m2, lane, LANES), axis=1, keepdims=True)
    e2 = jnp.exp(m2 - m1)
    w1 = 1.0 / (1.0 + e2)
    comb_ref[...] = jnp.where(lane == i1, w1, jnp.where(lane == i2, e2 * w1, 0.0))
    chosen = (lane == i1) | (lane == i2)
    onehot = jnp.where(chosen, 1.0, 0.0)
    earlier = lax.broadcasted_iota(jnp.int32, (tm, tm), 0) > lax.broadcasted_iota(jnp.int32, (tm, tm), 1)
    rank = _dot(jnp.where(earlier, 1.0, 0.0).astype(BF16), onehot.astype(BF16))
    code_ref[...] = jnp.where(chosen, rank, -1.0)
    cnt_ref[...] = jnp.broadcast_to(jnp.sum(onehot, axis=0, keepdims=True), cnt_ref.shape)


def _router(x, w_hi, w_lo, tm):
    n = x.shape[0]
    row = lambda i: (i, 0)
    return pl.pallas_call(
        _router_kernel,
        grid=(n // tm,),
        in_specs=[
            pl.BlockSpec((tm, D_MODEL), row),
            pl.BlockSpec(w_hi.shape, lambda i: (0, 0)),
            pl.BlockSpec(w_lo.shape, lambda i: (0, 0)),
        ],
        out_specs=[
            pl.BlockSpec((tm, LANES), row),
            pl.BlockSpec((tm, LANES), row),
            pl.BlockSpec((8, LANES), row),
        ],
        out_shape=[
            jax.ShapeDtypeStruct((n, LANES), F32),
            jax.ShapeDtypeStruct((n, LANES), F32),
            jax.ShapeDtypeStruct((n // tm * 8, LANES), F32),
        ],
        compiler_params=_params("parallel"),
        name="router",
    )(x, w_hi, w_lo)


def _gmlp_kernel(x_ref, win_ref, g_ref, b_ref, mix_ref, bias_ref, wout_ref, ln_g_ref, ln_b_ref,
                 o_ref, v_ref, *, tm, v_rows, v_every):
    x = x_ref[...]
    h = jax.nn.gelu(_dot(x.astype(BF16), win_ref[...]))
    u = h[:, :D_MODEL]
    v = _layer_norm(h[:, D_MODEL:], g_ref[...], b_ref[...])
    vb = v.astype(BF16)
    bias = bias_ref[...]
    gated = []
    for r in range(tm // GM_MIX):
        rows = slice(r * GM_MIX, (r + 1) * GM_MIX)
        cols = []
        for grp in range(GM_GROUPS):
            lanes = slice(grp * GM_GROUP_DIM, (grp + 1) * GM_GROUP_DIM)
            cols.append(_dot(mix_ref[grp], vb[rows, lanes]))
        mixed = jnp.concatenate(cols, axis=1) + bias
        gated.append((u[rows] * mixed).astype(BF16))
    y = _dot(jnp.concatenate(gated, axis=0), wout_ref[...])
    o_ref[...] = _layer_norm(ALPHA * x + y, ln_g_ref[...], ln_b_ref[...])

    @pl.when(pl.program_id(0) % v_every == v_every - 1)
    def _():
        v_ref[...] = v[tm - v_rows:, :]


def _gmlp(x, w, mix, bias, ln_g, ln_b, tm, v_rows, v_every):
    n = x.shape[0]
    row = lambda i: (i, 0)
    const = lambda i: (0, 0)
    n_v = n // (tm * v_every) * v_rows
    return pl.pallas_call(
        functools.partial(_gmlp_kernel, tm=tm, v_rows=v_rows, v_every=v_every),
        grid=(n // tm,),
        in_specs=[
            pl.BlockSpec((tm, D_MODEL), row),
            pl.BlockSpec(w['w_in'].shape, const),
            pl.BlockSpec((1, D_MODEL), const),
            pl.BlockSpec((1, D_MODEL), const),
            pl.BlockSpec(mix.shape, lambda i: (0, 0, 0)),
            pl.BlockSpec(bias.shape, const),
            pl.BlockSpec(w['w_out'].shape, const),
            pl.BlockSpec((1, D_MODEL), const),
            pl.BlockSpec((1, D_MODEL), const),
        ],
        out_specs=[
            pl.BlockSpec((tm, D_MODEL), row),
            pl.BlockSpec((v_rows, D_MODEL), lambda i: (i // v_every, 0)),
        ],
        out_shape=[
            jax.ShapeDtypeStruct((n, D_MODEL), F32),
            jax.ShapeDtypeStruct((n_v, D_MODEL), F32),
        ],
        compiler_params=_params("arbitrary"),
        name="gmlp",
    )(x, w['w_in'], w['g_norm'], w['b_norm'], mix, bias, w['w_out'], ln_g, ln_b)


def _ln_ple(z, g_ref, b_ref, p_ref, wpg_ref, wpp_ref):
    x2 = _layer_norm(z, g_ref[...], b_ref[...])
    gate = jax.nn.sigmoid(_dot(x2.astype(BF16), wpg_ref[...]))
    return x2 + gate * _dot(p_ref[...].astype(BF16), wpp_ref[...])


def _swiglu(xb, wg_ref, wu_ref, wd_ref):
    d_ff = wg_ref.shape[1]
    y = None
    for lo in range(0, d_ff, FF_CHUNK):
        hi = min(lo + FF_CHUNK, d_ff)
        h = jax.nn.silu(_dot(xb, wg_ref[:, lo:hi])) * _dot(xb, wu_ref[:, lo:hi])
        d = _dot(h.astype(BF16), wd_ref[lo:hi, :])
        y = d if y is None else y + d
    return y


def _attn_ffn_kernel(a_ref, x_ref, wo_ref, g1_ref, b1_ref, wg_ref, wu_ref, wd_ref, g_ref, b_ref, p_ref,
                     wpg_ref, wpp_ref, o_ref):
    x1 = _layer_norm(ALPHA * x_ref[...] + _dot(a_ref[...], wo_ref[...]), g1_ref[...], b1_ref[...])
    y = _swiglu(x1.astype(BF16), wg_ref, wu_ref, wd_ref)
    o_ref[...] = _ln_ple(ALPHA * x1 + y, g_ref, b_ref, p_ref, wpg_ref, wpp_ref)


def _resident(shape):
    return pl.BlockSpec(shape, lambda i: (0,) * len(shape), pipeline_mode=pl.Buffered(1))


def _attn_ffn(a, x, w_out, ln1_g, ln1_b, wg, wu, wd, ln_g, ln_b, p, layer, w_ple_gate, w_ple_proj, tm):
    n = x.shape[0]
    row = lambda i: (i, 0)
    const = lambda i: (0, 0)
    return pl.pallas_call(
        _attn_ffn_kernel,
        grid=(n // tm,),
        in_specs=[
            pl.BlockSpec((tm, a.shape[1]), row),
            pl.BlockSpec((tm, D_MODEL), row),
            _resident(w_out.shape),
            pl.BlockSpec((1, D_MODEL), const),
            pl.BlockSpec((1, D_MODEL), const),
            _resident(wg.shape),
            _resident(wu.shape),
            _resident(wd.shape),
            pl.BlockSpec((1, D_MODEL), const),
            pl.BlockSpec((1, D_MODEL), const),
            pl.BlockSpec((None, tm, p.shape[2]), lambda i: (layer, i, 0)),
            _resident(w_ple_gate.shape),
            _resident(w_ple_proj.shape),
        ],
        out_specs=pl.BlockSpec((tm, D_MODEL), row),
        out_shape=jax.ShapeDtypeStruct((n, D_MODEL), F32),
        compiler_params=_params("parallel"),
        name="attn_ffn",
    )(a, x, w_out, ln1_g, ln1_b, wg, wu, wd, ln_g, ln_b, p, w_ple_gate, w_ple_proj)


def _moe_plan(cnt, n_tiles, tm_ffn, n_ffn_tiles):
    cnt = cnt.reshape(n_tiles, 8, LANES)[:, 0, :N_EXPERTS].astype(jnp.int32)
    nch = (cnt + MOE_CHUNK - 1) // MOE_CHUNK
    base = jnp.cumsum(nch, axis=1) - nch
    seg = (cnt + BF16_ROWS - 1) // BF16_ROWS * BF16_ROWS
    group_rows = jnp.sum(seg, axis=0) + MOE_CHUNK
    group_pad = (group_rows + tm_ffn - 1) // tm_ffn * tm_ffn
    group_start = jnp.cumsum(group_pad) - group_pad
    seg_start = group_start[None, :] + jnp.cumsum(seg, axis=0) - seg
    tile_end = jnp.cumsum(group_pad // tm_ffn)
    n_used = tile_end[-1]
    tile_id = jnp.minimum(jnp.arange(n_ffn_tiles), n_used - 1)
    tile_expert = jnp.sum(tile_id[:, None] >= tile_end[None, :], axis=1)
    base_rows = jnp.zeros((n_tiles, 8, LANES), F32).at[:, :, :N_EXPERTS].set(
        (base * MOE_CHUNK).astype(F32)[:, None, :]).reshape(n_tiles * 8, LANES)
    flat = lambda a: a.reshape(-1).astype(jnp.int32)
    return dict(seg=flat(seg_start), nch=flat(nch), base=flat(base), base_rows=base_rows,
                tile_expert=flat(tile_expert), n_used=flat(n_used))


def _staged_rows(code, base_rows):
    pos = jnp.where(code >= 0.0, base_rows + code, -1.0)
    first = jnp.max(pos, axis=1, keepdims=True)
    second = jnp.max(jnp.where(pos == first, -1.0, pos), axis=1, keepdims=True)
    return pos, first, second


def _chunk_copies(hbm_ref, vmem_ref, sem, seg_ref, nch_ref, base_ref, tile, slot, to_hbm, action):
    for e in range(N_EXPERTS):
        idx = tile * N_EXPERTS + e
        for c in range(TOKEN_TILE // MOE_CHUNK):
            @pl.when(c < nch_ref[idx])
            def _():
                stage = vmem_ref.at[slot, pl.ds(pl.multiple_of((base_ref[idx] + c) * MOE_CHUNK, MOE_CHUNK), MOE_CHUNK)]
                group = hbm_ref.at[pl.ds(pl.multiple_of(seg_ref[idx] + c * MOE_CHUNK, BF16_ROWS), MOE_CHUNK)]
                src, dst = (stage, group) if to_hbm else (group, stage)
                copy = pltpu.make_async_copy(src, dst, sem.at[slot])
                copy.start() if action == "start" else copy.wait()


def _dispatch_kernel(seg_ref, nch_ref, base_ref, x_ref, code_ref, brow_ref, zeros_ref, xs_ref, stage_ref, sem):
    del zeros_ref
    t = pl.program_id(0)
    slot = t % 2
    _, first, second = _staged_rows(code_ref[...], brow_ref[0:1, :])
    lane = lax.broadcasted_iota(jnp.int32, (TOKEN_TILE, LANES), 1)
    rows_t = jnp.where(lane == 0, first, jnp.where(lane == 1, second, -1.0)).T
    r1, r2 = rows_t[0:1, :], rows_t[1:2, :]
    xb = x_ref[...].astype(BF16)
    n_chunks = base_ref[t * N_EXPERTS + N_EXPERTS - 1] + nch_ref[t * N_EXPERTS + N_EXPERTS - 1]
    for g in range(MOE_STAGE // MOE_GROUP):
        @pl.when(g * (MOE_GROUP // MOE_CHUNK) < n_chunks)
        def _():
            row = (lax.broadcasted_iota(jnp.int32, (MOE_GROUP, TOKEN_TILE), 0) + g * MOE_GROUP).astype(F32)
            sel = jnp.where((row == r1) | (row == r2), 1.0, 0.0).astype(BF16)
            stage_ref[slot, g * MOE_GROUP:(g + 1) * MOE_GROUP, :] = _dot(sel, xb).astype(BF16)

    copies = functools.partial(_chunk_copies, xs_ref, stage_ref, sem, seg_ref, nch_ref, base_ref, to_hbm=True)

    @pl.when(t > 0)
    def _():
        copies(tile=t - 1, slot=1 - slot, action="wait")

    copies(tile=t, slot=slot, action="start")

    @pl.when(t == pl.num_programs(0) - 1)
    def _():
        copies(tile=t, slot=slot, action="wait")


def _dispatch(x, code, plan, n_rows):
    n = x.shape[0]
    row = lambda i, *_: (i, 0)
    grid_spec = pltpu.PrefetchScalarGridSpec(
        num_scalar_prefetch=3,
        grid=(n // TOKEN_TILE,),
        in_specs=[
            pl.BlockSpec((TOKEN_TILE, D_MODEL), row),
            pl.BlockSpec((TOKEN_TILE, LANES), row),
            pl.BlockSpec((8, LANES), row),
            pl.BlockSpec(memory_space=pl.ANY),
        ],
        out_specs=pl.BlockSpec(memory_space=pl.ANY),
        scratch_shapes=[pltpu.VMEM((2, MOE_STAGE, D_MODEL), BF16), pltpu.SemaphoreType.DMA((2,))],
    )
    return pl.pallas_call(
        _dispatch_kernel,
        grid_spec=grid_spec,
        out_shape=jax.ShapeDtypeStruct((n_rows, D_MODEL), BF16),
        input_output_aliases={6: 0},
        compiler_params=_params("arbitrary"),
        name="moe_dispatch",
    )(plan['seg'], plan['nch'], plan['base'], x, code, plan['base_rows'], jnp.zeros((n_rows, D_MODEL), BF16))


def _moe_ffn_kernel(te_ref, nu_ref, x_ref, wg_ref, wu_ref, wd_ref, o_ref, acc_ref):
    del te_ref
    f = pl.program_id(1)
    used = pl.program_id(0) < nu_ref[0]

    @pl.when(jnp.logical_not(used) & (f == 0))
    def _():
        o_ref[...] = jnp.zeros(o_ref.shape, o_ref.dtype)

    @pl.when(used)
    def _():
        @pl.when(f == 0)
        def _():
            acc_ref[...] = jnp.zeros(acc_ref.shape, F32)

        acc_ref[...] += _swiglu(x_ref[...], wg_ref, wu_ref, wd_ref)

        @pl.when(f == pl.num_programs(1) - 1)
        def _():
            o_ref[...] = acc_ref[...].astype(o_ref.dtype)


def _moe_ffn(xs, plan, wg, wu, wd, tm, tf):
    n_rows = xs.shape[0]
    n_f = wg.shape[-1] // tf
    tile = lambda i, f, te, nu: (jnp.minimum(i, nu[0] - 1), 0)
    f_idx = lambda i, f, nu: jnp.where(i < nu[0], f, n_f - 1)
    grid_spec = pltpu.PrefetchScalarGridSpec(
        num_scalar_prefetch=2,
        grid=(n_rows // tm, n_f),
        in_specs=[
            pl.BlockSpec((tm, D_MODEL), tile),
            pl.BlockSpec((None, D_MODEL, tf), lambda i, f, te, nu: (te[i], 0, f_idx(i, f, nu))),
            pl.BlockSpec((None, D_MODEL, tf), lambda i, f, te, nu: (te[i], 0, f_idx(i, f, nu))),
            pl.BlockSpec((None, tf, D_MODEL), lambda i, f, te, nu: (te[i], f_idx(i, f, nu), 0)),
        ],
        out_specs=pl.BlockSpec((tm, D_MODEL), lambda i, f, te, nu: (i, 0)),
        scratch_shapes=[pltpu.VMEM((tm, D_MODEL), F32)],
    )
    return pl.pallas_call(
        _moe_ffn_kernel,
        grid_spec=grid_spec,
        out_shape=jax.ShapeDtypeStruct((n_rows, D_MODEL), BF16),
        compiler_params=_params("arbitrary", "arbitrary"),
        name="moe_ffn",
    )(plan['tile_expert'], plan['n_used'], xs, wg, wu, wd)


def _combine_kernel(seg_ref, nch_ref, base_ref, x_ref, comb_ref, code_ref, brow_ref, ys_ref,
                    g_ref, b_ref, p_ref, wpg_ref, wpp_ref, o_ref, buf_ref, y_ref, sem):
    t = pl.program_id(0)
    slot = t % 2
    copies = functools.partial(_chunk_copies, ys_ref, buf_ref, sem, seg_ref, nch_ref, base_ref, to_hbm=False)

    @pl.when(t == 0)
    def _():
        buf_ref[...] = jnp.zeros(buf_ref.shape, BF16)
        copies(tile=t, slot=slot, action="start")

    @pl.when(t + 1 < pl.num_programs(0))
    def _():
        copies(tile=t + 1, slot=1 - slot, action="start")

    copies(tile=t, slot=slot, action="wait")

    comb = comb_ref[...]
    pos, first, second = _staged_rows(code_ref[...], brow_ref[0:1, :])
    w_first = jnp.max(jnp.where(pos == first, comb, 0.0), axis=1, keepdims=True)
    w_second = jnp.max(jnp.where(pos == second, comb, 0.0), axis=1, keepdims=True)
    y_ref[...] = jnp.zeros(y_ref.shape, F32)
    n_chunks = base_ref[t * N_EXPERTS + N_EXPERTS - 1] + nch_ref[t * N_EXPERTS + N_EXPERTS - 1]
    for g in range(MOE_STAGE // MOE_GROUP):
        @pl.when(g * (MOE_GROUP // MOE_CHUNK) < n_chunks)
        def _():
            row = (lax.broadcasted_iota(jnp.int32, (TOKEN_TILE, MOE_GROUP), 1) + g * MOE_GROUP).astype(F32)
            rows = buf_ref[slot, g * MOE_GROUP:(g + 1) * MOE_GROUP, :]
            y_ref[...] += (w_first * _dot(jnp.where(row == first, 1.0, 0.0).astype(BF16), rows)
                           + w_second * _dot(jnp.where(row == second, 1.0, 0.0).astype(BF16), rows))

    o_ref[...] = _ln_ple(ALPHA * x_ref[...] + y_ref[...], g_ref, b_ref, p_ref, wpg_ref, wpp_ref)


def _combine(x, comb, code, plan, ys, ln_g, ln_b, p, layer, w_ple_gate, w_ple_proj):
    n = x.shape[0]
    row = lambda i, *_: (i, 0)
    const = lambda i, *_: (0, 0)
    grid_spec = pltpu.PrefetchScalarGridSpec(
        num_scalar_prefetch=3,
        grid=(n // TOKEN_TILE,),
        in_specs=[
            pl.BlockSpec((TOKEN_TILE, D_MODEL), row),
            pl.BlockSpec((TOKEN_TILE, LANES), row),
            pl.BlockSpec((TOKEN_TILE, LANES), row),
            pl.BlockSpec((8, LANES), row),
            pl.BlockSpec(memory_space=pl.ANY),
            pl.BlockSpec((1, D_MODEL), const),
            pl.BlockSpec((1, D_MODEL), const),
            pl.BlockSpec((None, TOKEN_TILE, p.shape[2]), lambda i, *_: (layer, i, 0)),
            pl.BlockSpec(w_ple_gate.shape, const),
            pl.BlockSpec(w_ple_proj.shape, const),
        ],
        out_specs=pl.BlockSpec((TOKEN_TILE, D_MODEL), row),
        scratch_shapes=[pltpu.VMEM((2, MOE_STAGE, D_MODEL), BF16), pltpu.VMEM((TOKEN_TILE, D_MODEL), F32),
                        pltpu.SemaphoreType.DMA((2,))],
    )
    return pl.pallas_call(
        _combine_kernel,
        grid_spec=grid_spec,
        out_shape=jax.ShapeDtypeStruct((n, D_MODEL), F32),
        compiler_params=_params("arbitrary"),
        name="moe_combine",
    )(plan['seg'], plan['nch'], plan['base'], x, comb, code, plan['base_rows'], ys,
      ln_g, ln_b, p, w_ple_gate, w_ple_proj)


def _moe(x, comb, code, cnt, wg, wu, wd, ln_g, ln_b, p, layer, w_ple_gate, w_ple_proj):
    n = x.shape[0]
    n_tiles = n // TOKEN_TILE
    tm_ffn = 1024 if n >= 8192 else 256
    max_rows = TOP_K * n + n_tiles * N_EXPERTS * (BF16_ROWS - 1) + N_EXPERTS * (MOE_CHUNK + tm_ffn - 1)
    n_ffn_tiles = -(-max_rows // tm_ffn)
    plan = _moe_plan(cnt, n_tiles, tm_ffn, n_ffn_tiles)
    xs = _dispatch(x, code, plan, n_ffn_tiles * tm_ffn)
    ys = _moe_ffn(xs, plan, wg, wu, wd, tm_ffn, MOE_FF_BLOCK)
    return _combine(x, comb, code, plan, ys, ln_g, ln_b, p, layer, w_ple_gate, w_ple_proj)


def _rope_tables(pos):
    inv = ROPE_THETA ** (-jnp.arange(ROPE_HALF, dtype=F32) / ROPE_HALF)
    ang = pos.astype(F32)[:, None] * inv[None, :]
    cos, sin, zero = jnp.cos(ang), jnp.sin(ang), jnp.zeros_like(ang)
    return (jnp.concatenate([cos, cos, zero, zero], axis=1),
            jnp.concatenate([-sin, sin, zero, zero], axis=1))


def _prep_mla(w_in, g_q, w_q_up, g_kv, w_kv_up, w_out):
    k_r = w_in[:, Q_LORA + KV_LORA:]
    w_in_p = jnp.concatenate([w_in[:, :Q_LORA + KV_LORA], k_r, k_r], axis=1)
    wq = w_q_up.reshape(Q_LORA, N_HEADS, QK_NOPE + QK_ROPE)
    wq_p = jnp.concatenate([wq, wq[:, :, QK_NOPE:]], axis=2).reshape(Q_LORA, N_HEADS * HEAD_PAD)
    wkv = w_kv_up.reshape(KV_LORA, N_HEADS, QK_NOPE + V_HEAD)
    w_k = wkv[:, :, :QK_NOPE].reshape(KV_LORA, -1).astype(BF16)
    w_v = wkv[:, :, QK_NOPE:].reshape(KV_LORA, -1).astype(BF16)
    return dict(w_in=w_in_p.astype(BF16), g_q=g_q[None, :], w_q_up=wq_p.astype(BF16), g_kv=g_kv[None, :],
                w_k=w_k, w_v=w_v, w_v_t=w_v.T, w_out=w_out.astype(BF16))


def _prep_gmlp_mix(w_s, b_s, chunk):
    wm = jnp.tril(w_s[:, :chunk, :chunk])
    reps = GM_MIX // chunk
    mix = jnp.einsum('ab,gts->gatbs', jnp.eye(reps, dtype=F32), wm).reshape(GM_GROUPS, GM_MIX, GM_MIX)
    bias = jnp.repeat(jnp.tile(b_s[:, :chunk].T, (reps, 1)), GM_GROUP_DIM, axis=1)
    return mix.astype(BF16), bias


def _row(v):
    return v[None, :]


def _trunk(x, p, past_latent, past_krope, W):
    b, s, _ = x.shape
    n = b * s
    tm = TOKEN_TILE
    past = 0 if past_latent is None else past_latent.shape[2]
    cos_t, sin_t = _rope_tables(past + jnp.arange(s))
    if s < tm:
        cos_t, sin_t = jnp.tile(cos_t, (tm // s, 1)), jnp.tile(sin_t, (tm // s, 1))
    x = x.reshape(n, D_MODEL)
    p = p.reshape(DEPTH, n, -1)
    chunk = min(s, GM_CHUNK)
    new_lat, new_kr, new_v = [], [], []
    for i in range(DEPTH):
        j = i // 2
        ln1 = (_row(W['ln1_g'][i]), _row(W['ln1_b'][i]))
        tail = (_row(W['ln2_g'][i]), _row(W['ln2_b'][i]), p, i, W['w_ple_gate'][i], W['w_ple_proj'][i])
        if i % 2 == 0:
            w = W['mla'][j]
            q, lat, kr, krp = _mla_proj(x, w, cos_t, sin_t, tm)
            lat = lat.reshape(b, s, KV_LORA)
            krp = krp.reshape(b, s, LANES)
            new_lat.append(lat)
            new_kr.append(kr.reshape(b, s, QK_ROPE))
            q = q.reshape(b, s, -1)
            if past_latent is None:
                k, vt = _kv_up(lat, krp, w, tm, True)
                att = _attn_prefill(q, k, vt)
            else:
                t_valid = past + s
                t_pad = -(-t_valid // LANES) * LANES
                lat_all = jnp.concatenate(
                    [past_latent[j], lat, jnp.zeros((b, t_pad - t_valid, KV_LORA), F32)], axis=1)
                krp_all = jnp.concatenate(
                    [jnp.pad(past_krope[j], ((0, 0), (0, 0), (0, LANES - QK_ROPE))), krp,
                     jnp.zeros((b, t_pad - t_valid, LANES), F32)], axis=1)
                k, v = _kv_up(lat_all, krp_all, w, t_pad, False)
                att = _attn_decode(q, k, v, past, t_valid)
            x = _attn_ffn(att.reshape(n, -1), x, w['w_out'], *ln1,
                          W['w_ffn_gate'][j], W['w_ffn_up'][j], W['w_ffn_down'][j], *tail, tm)
        else:
            w = W['gm'][j]
            mix, bias = _prep_gmlp_mix(w['w_spatial'], w['b_spatial'], chunk)
            v_rows, v_every = (GM_CHUNK, s // tm) if s >= GM_CHUNK else (tm, 1)
            x, v_new = _gmlp(x, w, mix, bias, *ln1, tm, v_rows, v_every)
            new_v.append(v_new.reshape(b, chunk, D_MODEL))
            comb, code, cnt = _router(x, W['w_router_hi'][j], W['w_router_lo'][j], tm)
            x = _moe(x, comb, code, cnt, W['w_moe_gate'][j], W['w_moe_up'][j], W['w_moe_down'][j], *tail)
    return x.reshape(b, s, D_MODEL), jnp.stack(new_lat), jnp.stack(new_kr), jnp.stack(new_v)


def kernel(x_prompt, x_sample, cache_mla_latent, cache_mla_krope, p_prompt, p_sample, w_mla_in, g_q_norm, w_q_up, g_kv_norm, w_kv_up, w_mla_out, w_gm_in, g_gm_norm, b_gm_norm, w_gm_spatial, b_gm_spatial, w_gm_out, w_ffn_gate, w_ffn_up, w_ffn_down, w_router, w_moe_gate, w_moe_up, w_moe_down, ln1_g, ln1_b, ln2_g, ln2_b, w_ple_gate, w_ple_proj):
    n_mla, n_gm = w_mla_in.shape[0], w_gm_in.shape[0]
    w_router = jnp.pad(w_router, ((0, 0), (0, 0), (0, LANES - N_EXPERTS)))
    w_router_hi = w_router.astype(BF16)
    W = dict(
        mla=[_prep_mla(w_mla_in[j], g_q_norm[j], w_q_up[j], g_kv_norm[j], w_kv_up[j], w_mla_out[j])
             for j in range(n_mla)],
        gm=[dict(w_in=w_gm_in[j].astype(BF16), g_norm=_row(g_gm_norm[j]), b_norm=_row(b_gm_norm[j]),
                 w_spatial=w_gm_spatial[j], b_spatial=b_gm_spatial[j], w_out=w_gm_out[j].astype(BF16))
            for j in range(n_gm)],
        w_ffn_gate=w_ffn_gate.astype(BF16), w_ffn_up=w_ffn_up.astype(BF16), w_ffn_down=w_ffn_down.astype(BF16),
        w_router_hi=w_router_hi, w_router_lo=(w_router - w_router_hi.astype(F32)).astype(BF16),
        w_moe_gate=w_moe_gate.astype(BF16), w_moe_up=w_moe_up.astype(BF16), w_moe_down=w_moe_down.astype(BF16),
        ln1_g=ln1_g, ln1_b=ln1_b, ln2_g=ln2_g, ln2_b=ln2_b,
        w_ple_gate=w_ple_gate.astype(BF16), w_ple_proj=w_ple_proj.astype(BF16),
    )
    y_p, lat_p, kr_p, v_p = _trunk(x_prompt, p_prompt, None, None, W)
    y_s, lat_s, kr_s, v_s = _trunk(x_sample, p_sample, cache_mla_latent, cache_mla_krope, W)
    return (y_p, y_s, lat_p, kr_p, lat_s, kr_s, v_p, v_s)
```

```python
import functools

import jax
import jax.numpy as jnp
from jax import lax
from jax.experimental import pallas as pl
from jax.experimental.pallas import tpu as pltpu

F32 = jnp.float32
BF16 = jnp.bfloat16

D_MODEL = 1024
DEPTH = 4
CHUNK = 64
N_HEADS = 8
QK_NOPE = 128
QK_ROPE = 64
V_HEAD = 128
Q_LORA = 384
KV_LORA = 256
ROPE_THETA = 10000.0
ATTN_SCALE = (QK_NOPE + QK_ROPE) ** -0.5
GM_CHUNK = 128
GM_GROUPS = 8
GM_GROUP_DIM = D_MODEL // GM_GROUPS
N_EXPERTS = 8
TOP_K = 2
ALPHA = (2 * DEPTH) ** 0.25

LANES = 128
BF16_ROWS = 16
HEAD_PAD = 2 * LANES
ROPE_HALF = QK_ROPE // 2
LOG2E = 1.4426950408889634
Q_SCALE = ATTN_SCALE * LOG2E
ATT_TILE = 256
ATT_ROWS = 32
GM_MIX = 256
FF_CHUNK = 1024
MOE_FF_BLOCK = 1792
TOKEN_TILE = 512
MOE_CHUNK = 128
MOE_SLOTS = TOP_K * TOKEN_TILE // MOE_CHUNK + N_EXPERTS
MOE_STAGE = MOE_SLOTS * MOE_CHUNK
MOE_GROUP = 512
VMEM_LIMIT = 56 * 1024 * 1024


def _params(*sem):
    return pltpu.CompilerParams(dimension_semantics=sem, vmem_limit_bytes=VMEM_LIMIT)


def _dot(a, b):
    return jnp.dot(a, b, preferred_element_type=F32)


def _layer_norm(z, g, b):
    mu = jnp.mean(z, -1, keepdims=True)
    zc = z - mu
    var = jnp.mean(zc * zc, -1, keepdims=True)
    return zc * lax.rsqrt(var + 1e-5) * g + b


def _rms_norm(z, g):
    return z * lax.rsqrt(jnp.mean(z * z, -1, keepdims=True) + 1e-6) * g


def _rope(r, cos_t, sin_t):
    return r * cos_t + pltpu.roll(r, ROPE_HALF, 1) * sin_t


def _write_kv(lat, krp, wk_ref, wv_ref, k_ref, v_ref, transpose_v):
    kn = _dot(lat, wk_ref[...])
    for hd in range(N_HEADS):
        lo = hd * HEAD_PAD
        k_ref[:, lo:lo + LANES] = kn[:, hd * LANES:(hd + 1) * LANES].astype(BF16)
        k_ref[:, lo + LANES:lo + HEAD_PAD] = krp
    if transpose_v:
        v_ref[...] = lax.dot_general(wv_ref[...], lat, (((1,), (1,)), ((), ())),
                                     preferred_element_type=F32).astype(BF16)
    else:
        v_ref[...] = _dot(lat, wv_ref[...]).astype(BF16)


def _mla_proj_kernel(*refs, with_kv):
    x_ref, win_ref, gq_ref, wq_ref, gkv_ref, cos_ref, sin_ref = refs[:7]
    if with_kv:
        wk_ref, wvt_ref, q_ref, lat_ref, kr_ref, k_ref, vt_ref = refs[7:]
    else:
        q_ref, lat_ref, kr_ref, krp_ref = refs[7:]
    xb = x_ref[...].astype(BF16)
    h = _dot(xb, win_ref[...])
    cos_t = cos_ref[...]
    sin_t = sin_ref[...]
    cq = _rms_norm(h[:, :Q_LORA], gq_ref[...])
    q = _dot(cq.astype(BF16), wq_ref[...])
    for hd in range(N_HEADS):
        lo = hd * HEAD_PAD
        q_ref[:, lo:lo + LANES] = (q[:, lo:lo + LANES] * Q_SCALE).astype(BF16)
        rot = _rope(q[:, lo + LANES:lo + HEAD_PAD], cos_t, sin_t) * Q_SCALE
        q_ref[:, lo + LANES:lo + HEAD_PAD] = rot.astype(BF16)
    lat = _rms_norm(h[:, Q_LORA:Q_LORA + KV_LORA], gkv_ref[...])
    lat_ref[...] = lat
    krot = _rope(h[:, Q_LORA + KV_LORA:], cos_t, sin_t)
    kr_ref[...] = krot[:, :QK_ROPE]
    if with_kv:
        _write_kv(lat.astype(BF16), krot.astype(BF16), wk_ref, wvt_ref, k_ref, vt_ref, True)
    else:
        krp_ref[...] = krot


def _mla_proj(x, w, cos_t, sin_t, tm, with_kv):
    n = x.shape[0]
    per_seq = cos_t.shape[0] // tm
    row = lambda i: (i, 0)
    const = lambda i: (0, 0)
    in_specs = [
        pl.BlockSpec((tm, D_MODEL), row),
        pl.BlockSpec(w['w_in'].shape, const),
        pl.BlockSpec((1, Q_LORA), const),
        pl.BlockSpec(w['w_q_up'].shape, const),
        pl.BlockSpec((1, KV_LORA), const),
        pl.BlockSpec((tm, LANES), lambda i: (i % per_seq, 0)),
        pl.BlockSpec((tm, LANES), lambda i: (i % per_seq, 0)),
    ]
    args = [x, w['w_in'], w['g_q'], w['w_q_up'], w['g_kv'], cos_t, sin_t]
    out_specs = [
        pl.BlockSpec((tm, N_HEADS * HEAD_PAD), row),
        pl.BlockSpec((tm, KV_LORA), row),
        pl.BlockSpec((tm, QK_ROPE), row),
    ]
    out_shape = [
        jax.ShapeDtypeStruct((n, N_HEADS * HEAD_PAD), BF16),
        jax.ShapeDtypeStruct((n, KV_LORA), F32),
        jax.ShapeDtypeStruct((n, QK_ROPE), F32),
    ]
    if with_kv:
        in_specs += [pl.BlockSpec(w['w_k'].shape, const), pl.BlockSpec(w['w_v_t'].shape, const)]
        args += [w['w_k'], w['w_v_t']]
        out_specs += [pl.BlockSpec((tm, N_HEADS * HEAD_PAD), row),
                      pl.BlockSpec((None, N_HEADS * V_HEAD, tm), lambda i: (i // per_seq, 0, i % per_seq))]
        out_shape += [jax.ShapeDtypeStruct((n, N_HEADS * HEAD_PAD), BF16),
                      jax.ShapeDtypeStruct((n // (per_seq * tm), N_HEADS * V_HEAD, per_seq * tm), BF16)]
    else:
        out_specs.append(pl.BlockSpec((tm, LANES), row))
        out_shape.append(jax.ShapeDtypeStruct((n, LANES), F32))
    return pl.pallas_call(
        functools.partial(_mla_proj_kernel, with_kv=with_kv),
        grid=(n // tm,),
        in_specs=in_specs,
        out_specs=out_specs,
        out_shape=out_shape,
        compiler_params=_params("parallel"),
        name="mla_proj",
    )(*args)


def _kv_up_kernel(lat_ref, krp_ref, wk_ref, wv_ref, k_ref, v_ref):
    _write_kv(lat_ref[...].astype(BF16), krp_ref[...].astype(BF16), wk_ref, wv_ref, k_ref, v_ref, False)


def _kv_up(lat, krp, w, tt):
    b, t, _ = lat.shape
    blk = lambda bi, i: (bi, i, 0)
    return pl.pallas_call(
        _kv_up_kernel,
        grid=(b, t // tt),
        in_specs=[
            pl.BlockSpec((None, tt, KV_LORA), blk),
            pl.BlockSpec((None, tt, LANES), blk),
            pl.BlockSpec(w['w_k'].shape, lambda bi, i: (0, 0)),
            pl.BlockSpec(w['w_v'].shape, lambda bi, i: (0, 0)),
        ],
        out_specs=[pl.BlockSpec((None, tt, N_HEADS * HEAD_PAD), blk), pl.BlockSpec((None, tt, N_HEADS * V_HEAD), blk)],
        out_shape=[jax.ShapeDtypeStruct((b, t, N_HEADS * HEAD_PAD), BF16),
                   jax.ShapeDtypeStruct((b, t, N_HEADS * V_HEAD), BF16)],
        compiler_params=_params("parallel", "parallel"),
        name="kv_up",
    )(lat, krp, w['w_k'], w['w_v'])


def _scores(a, b):
    return lax.dot_general(a, b, (((1,), (1,)), ((), ())), preferred_element_type=F32)


def _tree(op, parts):
    while len(parts) > 1:
        parts = [op(parts[i], parts[i + 1]) if i + 1 < len(parts) else parts[i] for i in range(0, len(parts), 2)]
    return parts[0]


def _attn_prefill_kernel(q_ref, k_ref, vt_ref, o_ref, s_ref, p_ref):
    t = ATT_TILE
    n_q = q_ref.shape[0] // t
    k_chunk = lax.broadcasted_iota(jnp.int32, (t, t), 0) // CHUNK
    q_chunk = lax.broadcasted_iota(jnp.int32, (t, t), 1) // CHUNK
    diag_visible = k_chunk <= q_chunk

    def scores(qi):
        lo, hi = qi * t, (qi + 1) * t
        s_ref[qi % 2, 0:hi, :] = _scores(k_ref[0:hi, :], q_ref[lo:hi, :])

    def softmax(qi):
        lo, slot = qi * t, qi % 2
        s = [s_ref[slot, r:r + ATT_ROWS, :] for r in range(0, lo, ATT_ROWS)] + [
            jnp.where(diag_visible[r:r + ATT_ROWS], s_ref[slot, lo + r:lo + r + ATT_ROWS, :], -jnp.inf)
            for r in range(0, t, ATT_ROWS)]
        m = jnp.max(_tree(jnp.maximum, s), axis=0, keepdims=True)
        p = [jnp.exp2(blk - m) for blk in s]
        for i, blk in enumerate(p):
            p_ref[slot, i * ATT_ROWS:(i + 1) * ATT_ROWS, :] = blk.astype(BF16)
        return jnp.sum(_tree(jnp.add, p), axis=0, keepdims=True)

    def values(qi, l):
        lo, hi = qi * t, (qi + 1) * t
        acc = _dot(vt_ref[:, 0:hi], p_ref[qi % 2, 0:hi, :])
        o_ref[lo:hi, :] = (acc / l).T.astype(o_ref.dtype)

    scores(0)
    for qi in range(n_q):
        if qi + 1 < n_q:
            scores(qi + 1)
        values(qi, softmax(qi))


def _attn_prefill(q, k, vt):
    b, s, _ = q.shape
    return pl.pallas_call(
        _attn_prefill_kernel,
        grid=(b, N_HEADS),
        in_specs=[
            pl.BlockSpec((None, s, HEAD_PAD), lambda bi, h: (bi, 0, h)),
            pl.BlockSpec((None, s, HEAD_PAD), lambda bi, h: (bi, 0, h)),
            pl.BlockSpec((None, V_HEAD, s), lambda bi, h: (bi, h, 0)),
        ],
        out_specs=pl.BlockSpec((None, s, V_HEAD), lambda bi, h: (bi, 0, h)),
        out_shape=jax.ShapeDtypeStruct((b, s, N_HEADS * V_HEAD), BF16),
        scratch_shapes=[pltpu.VMEM((2, s, ATT_TILE), F32), pltpu.VMEM((2, s, ATT_TILE), BF16)],
        compiler_params=_params("parallel", "parallel"),
        name="attn_prefill",
    )(q, k, vt)


def _attn_decode_kernel(q_ref, k_ref, v_ref, o_ref, *, q_start, t_valid):
    s_q, t = q_ref.shape[0], k_ref.shape[0]
    q_chunk = (q_start + lax.broadcasted_iota(jnp.int32, (s_q, t), 0)) // CHUNK
    k_pos = lax.broadcasted_iota(jnp.int32, (s_q, t), 1)
    visible = (k_pos // CHUNK <= q_chunk) & (k_pos < t_valid)
    for hd in range(N_HEADS):
        qk = slice(hd * HEAD_PAD, (hd + 1) * HEAD_PAD)
        vo = slice(hd * V_HEAD, (hd + 1) * V_HEAD)
        s = jnp.where(visible, _scores(q_ref[:, qk], k_ref[:, qk]), -jnp.inf)
        p = jnp.exp2(s - jnp.max(s, axis=1, keepdims=True))
        l = jnp.sum(p, axis=1, keepdims=True)
        o_ref[:, vo] = (_dot(p.astype(BF16), v_ref[:, vo]) / l).astype(o_ref.dtype)


def _attn_decode(q, k, v, q_start, t_valid):
    b, s, _ = q.shape
    t = k.shape[1]
    seq = lambda bi: (bi, 0, 0)
    return pl.pallas_call(
        functools.partial(_attn_decode_kernel, q_start=q_start, t_valid=t_valid),
        grid=(b,),
        in_specs=[
            pl.BlockSpec((None, s, N_HEADS * HEAD_PAD), seq),
            pl.BlockSpec((None, t, N_HEADS * HEAD_PAD), seq),
            pl.BlockSpec((None, t, N_HEADS * V_HEAD), seq),
        ],
        out_specs=pl.BlockSpec((None, s, N_HEADS * V_HEAD), seq),
        out_shape=jax.ShapeDtypeStruct((b, s, N_HEADS * V_HEAD), BF16),
        compiler_params=_params("parallel"),
        name="attn_decode",
    )(q, k, v)


def _router_kernel(x_ref, wh_ref, wl_ref, comb_ref, code_ref, cnt_ref):
    x = x_ref[...]
    xh = x.astype(BF16)
    xl = (x - xh.astype(F32)).astype(BF16)
    wh = wh_ref[...]
    logits = _dot(xh, wh) + (_dot(xl, wh) + _dot(xh, wl_ref[...]))
    tm = logits.shape[0]
    lane = lax.broadcasted_iota(jnp.int32, logits.shape, 1)
    logits = jnp.where(lane < N_EXPERTS, logits, -jnp.inf)
    m1 = jnp.max(logits, axis=1, keepdims=True)
    i1 = jnp.min(jnp.where(logits == m1, lane, LANES), axis=1, keepdims=True)
    rest = jnp.where(lane == i1, -jnp.inf, logits)
    m2 = jnp.max(rest, axis=1, keepdims=True)
    i2 = jnp.min(jnp.where(rest == m2, lane, LANES), axis=1, keepdims=True)
    e2 = jnp.exp(m2 - m1)
    w1 = 1.0 / (1.0 + e2)
    comb_ref[...] = jnp.where(lane == i1, w1, jnp.where(lane == i2, e2 * w1, 0.0))
    chosen = (lane == i1) | (lane == i2)
    onehot = jnp.where(chosen, 1.0, 0.0)
    earlier = lax.broadcasted_iota(jnp.int32, (tm, tm), 0) > lax.broadcasted_iota(jnp.int32, (tm, tm), 1)
    rank = _dot(jnp.where(earlier, 1.0, 0.0).astype(BF16), onehot.astype(BF16))
    code_ref[...] = jnp.where(chosen, rank, -1.0)
    cnt_ref[...] = jnp.broadcast_to(jnp.sum(onehot, axis=0, keepdims=True), cnt_ref.shape)


def _router(x, w_hi, w_lo, tm):
    n = x.shape[0]
    row = lambda i: (i, 0)
    return pl.pallas_call(
        _router_kernel,
        grid=(n // tm,),
        in_specs=[
            pl.BlockSpec((tm, D_MODEL), row),
            pl.BlockSpec(w_hi.shape, lambda i: (0, 0)),
            pl.BlockSpec(w_lo.shape, lambda i: (0, 0)),
        ],
        out_specs=[
            pl.BlockSpec((tm, LANES), row),
            pl.BlockSpec((tm, LANES), row),
            pl.BlockSpec((8, LANES), row),
        ],
        out_shape=[
            jax.ShapeDtypeStruct((n, LANES), F32),
            jax.ShapeDtypeStruct((n, LANES), F32),
            jax.ShapeDtypeStruct((n // tm * 8, LANES), F32),
        ],
        compiler_params=_params("parallel"),
        name="router",
    )(x, w_hi, w_lo)


def _gmlp_kernel(x_ref, win_ref, g_ref, b_ref, mix_ref, bias_ref, wout_ref, ln_g_ref, ln_b_ref,
                 o_ref, v_ref, *, tm, v_rows, v_every):
    x = x_ref[...]
    h = jax.nn.gelu(_dot(x.astype(BF16), win_ref[...]))
    u = h[:, :D_MODEL]
    v = _layer_norm(h[:, D_MODEL:], g_ref[...], b_ref[...])
    vb = v.astype(BF16)
    bias = bias_ref[...]
    gated = []
    for r in range(tm // GM_MIX):
        rows = slice(r * GM_MIX, (r + 1) * GM_MIX)
        cols = []
        for grp in range(GM_GROUPS):
            lanes = slice(grp * GM_GROUP_DIM, (grp + 1) * GM_GROUP_DIM)
            cols.append(_dot(mix_ref[grp], vb[rows, lanes]))
        mixed = jnp.concatenate(cols, axis=1) + bias
        gated.append((u[rows] * mixed).astype(BF16))
    y = _dot(jnp.concatenate(gated, axis=0), wout_ref[...])
    o_ref[...] = _layer_norm(ALPHA * x + y, ln_g_ref[...], ln_b_ref[...])

    @pl.when(pl.program_id(0) % v_every == v_every - 1)
    def _():
        v_ref[...] = v[tm - v_rows:, :]


def _gmlp(x, w, mix, bias, ln_g, ln_b, tm, v_rows, v_every):
    n = x.shape[0]
    row = lambda i: (i, 0)
    const = lambda i: (0, 0)
    n_v = n // (tm * v_every) * v_rows
    return pl.pallas_call(
        functools.partial(_gmlp_kernel, tm=tm, v_rows=v_rows, v_every=v_every),
        grid=(n // tm,),
        in_specs=[
            pl.BlockSpec((tm, D_MODEL), row),
            pl.BlockSpec(w['w_in'].shape, const),
            pl.BlockSpec((1, D_MODEL), const),
            pl.BlockSpec((1, D_MODEL), const),
            pl.BlockSpec(mix.shape, lambda i: (0, 0, 0)),
            pl.BlockSpec(bias.shape, const),
            pl.BlockSpec(w['w_out'].shape, const),
            pl.BlockSpec((1, D_MODEL), const),
            pl.BlockSpec((1, D_MODEL), const),
        ],
        out_specs=[
            pl.BlockSpec((tm, D_MODEL), row),
            pl.BlockSpec((v_rows, D_MODEL), lambda i: (i // v_every, 0)),
        ],
        out_shape=[
            jax.ShapeDtypeStruct((n, D_MODEL), F32),
            jax.ShapeDtypeStruct((n_v, D_MODEL), F32),
        ],
        compiler_params=_params("arbitrary"),
        name="gmlp",
    )(x, w['w_in'], w['g_norm'], w['b_norm'], mix, bias, w['w_out'], ln_g, ln_b)


def _ln_ple(z, g_ref, b_ref, p_ref, wpg_ref, wpp_ref):
    x2 = _layer_norm(z, g_ref[...], b_ref[...])
    gate = jax.nn.sigmoid(_dot(x2.astype(BF16), wpg_ref[...]))
    return x2 + gate * _dot(p_ref[...].astype(BF16), wpp_ref[...])


def _swiglu(xb, wg_ref, wu_ref, wd_ref):
    d_ff = wg_ref.shape[1]
    y = None
    for lo in range(0, d_ff, FF_CHUNK):
        hi = min(lo + FF_CHUNK, d_ff)
        h = jax.nn.silu(_dot(xb, wg_ref[:, lo:hi])) * _dot(xb, wu_ref[:, lo:hi])
        d = _dot(h.astype(BF16), wd_ref[lo:hi, :])
        y = d if y is None else y + d
    return y


def _attn_ffn_kernel(a_ref, x_ref, wo_ref, g1_ref, b1_ref, wg_ref, wu_ref, wd_ref, g_ref, b_ref, p_ref,
                     wpg_ref, wpp_ref, o_ref):
    x1 = _layer_norm(ALPHA * x_ref[...] + _dot(a_ref[...], wo_ref[...]), g1_ref[...], b1_ref[...])
    y = _swiglu(x1.astype(BF16), wg_ref, wu_ref, wd_ref)
    o_ref[...] = _ln_ple(ALPHA * x1 + y, g_ref, b_ref, p_ref, wpg_ref, wpp_ref)


def _resident(shape):
    return pl.BlockSpec(shape, lambda i: (0,) * len(shape), pipeline_mode=pl.Buffered(1))


def _attn_ffn(a, x, w_out, ln1_g, ln1_b, wg, wu, wd, ln_g, ln_b, p, layer, w_ple_gate, w_ple_proj, tm):
    n = x.shape[0]
    row = lambda i: (i, 0)
    const = lambda i: (0, 0)
    return pl.pallas_call(
        _attn_ffn_kernel,
        grid=(n // tm,),
        in_specs=[
            pl.BlockSpec((tm, a.shape[1]), row),
            pl.BlockSpec((tm, D_MODEL), row),
            _resident(w_out.shape),
            pl.BlockSpec((1, D_MODEL), const),
            pl.BlockSpec((1, D_MODEL), const),
            _resident(wg.shape),
            _resident(wu.shape),
            _resident(wd.shape),
            pl.BlockSpec((1, D_MODEL), const),
            pl.BlockSpec((1, D_MODEL), const),
            pl.BlockSpec((None, tm, p.shape[2]), lambda i: (layer, i, 0)),
            _resident(w_ple_gate.shape),
            _resident(w_ple_proj.shape),
        ],
        out_specs=pl.BlockSpec((tm, D_MODEL), row),
        out_shape=jax.ShapeDtypeStruct((n, D_MODEL), F32),
        compiler_params=_params("parallel"),
        name="attn_ffn",
    )(a, x, w_out, ln1_g, ln1_b, wg, wu, wd, ln_g, ln_b, p, w_ple_gate, w_ple_proj)


def _moe_plan(cnt, n_tiles, tm_ffn, n_ffn_tiles):
    cnt = cnt.reshape(n_tiles, 8, LANES)[:, 0, :N_EXPERTS].astype(jnp.int32)
    nch = (cnt + MOE_CHUNK - 1) // MOE_CHUNK
    base = jnp.cumsum(nch, axis=1) - nch
    seg = (cnt + BF16_ROWS - 1) // BF16_ROWS * BF16_ROWS
    group_rows = jnp.sum(seg, axis=0) + MOE_CHUNK
    group_pad = (group_rows + tm_ffn - 1) // tm_ffn * tm_ffn
    group_start = jnp.cumsum(group_pad) - group_pad
    seg_start = group_start[None, :] + jnp.cumsum(seg, axis=0) - seg
    tile_end = jnp.cumsum(group_pad // tm_ffn)
    n_used = tile_end[-1]
    tile_id = jnp.minimum(jnp.arange(n_ffn_tiles), n_used - 1)
    tile_expert = jnp.sum(tile_id[:, None] >= tile_end[None, :], axis=1)
    base_rows = jnp.zeros((n_tiles, 8, LANES), F32).at[:, :, :N_EXPERTS].set(
        (base * MOE_CHUNK).astype(F32)[:, None, :]).reshape(n_tiles * 8, LANES)
    flat = lambda a: a.reshape(-1).astype(jnp.int32)
    return dict(seg=flat(seg_start), nch=flat(nch), base=flat(base), base_rows=base_rows,
                tile_expert=flat(tile_expert), n_used=flat(n_used))


def _staged_rows(code, base_rows):
    pos = jnp.where(code >= 0.0, base_rows + code, -1.0)
    first = jnp.max(pos, axis=1, keepdims=True)
    second = jnp.max(jnp.where(pos == first, -1.0, pos), axis=1, keepdims=True)
    return pos, first, second


def _chunk_copies(hbm_ref, vmem_ref, sem, seg_ref, nch_ref, base_ref, tile, slot, to_hbm, action):
    for e in range(N_EXPERTS):
        idx = tile * N_EXPERTS + e
        for c in range(TOKEN_TILE // MOE_CHUNK):
            @pl.when(c < nch_ref[idx])
            def _():
                stage = vmem_ref.at[slot, pl.ds(pl.multiple_of((base_ref[idx] + c) * MOE_CHUNK, MOE_CHUNK), MOE_CHUNK)]
                group = hbm_ref.at[pl.ds(pl.multiple_of(seg_ref[idx] + c * MOE_CHUNK, BF16_ROWS), MOE_CHUNK)]
                src, dst = (stage, group) if to_hbm else (group, stage)
                copy = pltpu.make_async_copy(src, dst, sem.at[slot])
                copy.start() if action == "start" else copy.wait()


def _dispatch_kernel(seg_ref, nch_ref, base_ref, x_ref, code_ref, brow_ref, zeros_ref, xs_ref, stage_ref, sem):
    del zeros_ref
    t = pl.program_id(0)
    slot = t % 2
    _, first, second = _staged_rows(code_ref[...], brow_ref[0:1, :])
    lane = lax.broadcasted_iota(jnp.int32, (TOKEN_TILE, LANES), 1)
    rows_t = jnp.where(lane == 0, first, jnp.where(lane == 1, second, -1.0)).T
    r1, r2 = rows_t[0:1, :], rows_t[1:2, :]
    xb = x_ref[...].astype(BF16)
    n_chunks = base_ref[t * N_EXPERTS + N_EXPERTS - 1] + nch_ref[t * N_EXPERTS + N_EXPERTS - 1]
    for g in range(MOE_STAGE // MOE_GROUP):
        @pl.when(g * (MOE_GROUP // MOE_CHUNK) < n_chunks)
        def _():
            row = (lax.broadcasted_iota(jnp.int32, (MOE_GROUP, TOKEN_TILE), 0) + g * MOE_GROUP).astype(F32)
            sel = jnp.where((row == r1) | (row == r2), 1.0, 0.0).astype(BF16)
            stage_ref[slot, g * MOE_GROUP:(g + 1) * MOE_GROUP, :] = _dot(sel, xb).astype(BF16)

    copies = functools.partial(_chunk_copies, xs_ref, stage_ref, sem, seg_ref, nch_ref, base_ref, to_hbm=True)

    @pl.when(t > 0)
    def _():
        copies(tile=t - 1, slot=1 - slot, action="wait")

    copies(tile=t, slot=slot, action="start")

    @pl.when(t == pl.num_programs(0) - 1)
    def _():
        copies(tile=t, slot=slot, action="wait")


def _dispatch(x, code, plan, n_rows):
    n = x.shape[0]
    row = lambda i, *_: (i, 0)
    grid_spec = pltpu.PrefetchScalarGridSpec(
        num_scalar_prefetch=3,
        grid=(n // TOKEN_TILE,),
        in_specs=[
            pl.BlockSpec((TOKEN_TILE, D_MODEL), row),
            pl.BlockSpec((TOKEN_TILE, LANES), row),
            pl.BlockSpec((8, LANES), row),
            pl.BlockSpec(memory_space=pl.ANY),
        ],
        out_specs=pl.BlockSpec(memory_space=pl.ANY),
        scratch_shapes=[pltpu.VMEM((2, MOE_STAGE, D_MODEL), BF16), pltpu.SemaphoreType.DMA((2,))],
    )
    return pl.pallas_call(
        _dispatch_kernel,
        grid_spec=grid_spec,
        out_shape=jax.ShapeDtypeStruct((n_rows, D_MODEL), BF16),
        input_output_aliases={6: 0},
        compiler_params=_params("arbitrary"),
        name="moe_dispatch",
    )(plan['seg'], plan['nch'], plan['base'], x, code, plan['base_rows'], jnp.zeros((n_rows, D_MODEL), BF16))


def _moe_ffn_kernel(te_ref, nu_ref, x_ref, wg_ref, wu_ref, wd_ref, o_ref, acc_ref):
    del te_ref
    f = pl.program_id(1)
    used = pl.program_id(0) < nu_ref[0]

    @pl.when(jnp.logical_not(used) & (f == 0))
    def _():
        o_ref[...] = jnp.zeros(o_ref.shape, o_ref.dtype)

    @pl.when(used)
    def _():
        @pl.when(f == 0)
        def _():
            acc_ref[...] = jnp.zeros(acc_ref.shape, F32)

        acc_ref[...] += _swiglu(x_ref[...], wg_ref, wu_ref, wd_ref)

        @pl.when(f == pl.num_programs(1) - 1)
        def _():
            o_ref[...] = acc_ref[...].astype(o_ref.dtype)


def _moe_ffn(xs, plan, wg, wu, wd, layer, tm, tf):
    n_rows = xs.shape[0]
    n_f = wg.shape[-1] // tf
    tile = lambda i, f, te, nu: (jnp.minimum(i, nu[0] - 1), 0)
    f_idx = lambda i, f, nu: jnp.where(i < nu[0], f, n_f - 1)
    grid_spec = pltpu.PrefetchScalarGridSpec(
        num_scalar_prefetch=2,
        grid=(n_rows // tm, n_f),
        in_specs=[
            pl.BlockSpec((tm, D_MODEL), tile),
            pl.BlockSpec((None, None, D_MODEL, tf), lambda i, f, te, nu: (layer, te[i], 0, f_idx(i, f, nu))),
            pl.BlockSpec((None, None, D_MODEL, tf), lambda i, f, te, nu: (layer, te[i], 0, f_idx(i, f, nu))),
            pl.BlockSpec((None, None, tf, D_MODEL), lambda i, f, te, nu: (layer, te[i], f_idx(i, f, nu), 0)),
        ],
        out_specs=pl.BlockSpec((tm, D_MODEL), lambda i, f, te, nu: (i, 0)),
        scratch_shapes=[pltpu.VMEM((tm, D_MODEL), F32)],
    )
    return pl.pallas_call(
        _moe_ffn_kernel,
        grid_spec=grid_spec,
        out_shape=jax.ShapeDtypeStruct((n_rows, D_MODEL), BF16),
        compiler_params=_params("arbitrary", "arbitrary"),
        name="moe_ffn",
    )(plan['tile_expert'], plan['n_used'], xs, wg, wu, wd)


def _combine_kernel(seg_ref, nch_ref, base_ref, x_ref, comb_ref, code_ref, brow_ref, ys_ref,
                    g_ref, b_ref, p_ref, wpg_ref, wpp_ref, o_ref, buf_ref, y_ref, sem):
    t = pl.program_id(0)
    slot = t % 2
    copies = functools.partial(_chunk_copies, ys_ref, buf_ref, sem, seg_ref, nch_ref, base_ref, to_hbm=False)

    @pl.when(t == 0)
    def _():
        buf_ref[...] = jnp.zeros(buf_ref.shape, BF16)
        copies(tile=t, slot=slot, action="start")

    @pl.when(t + 1 < pl.num_programs(0))
    def _():
        copies(tile=t + 1, slot=1 - slot, action="start")

    copies(tile=t, slot=slot, action="wait")

    comb = comb_ref[...]
    pos, first, second = _staged_rows(code_ref[...], brow_ref[0:1, :])
    w_first = jnp.max(jnp.where(pos == first, comb, 0.0), axis=1, keepdims=True)
    w_second = jnp.max(jnp.where(pos == second, comb, 0.0), axis=1, keepdims=True)
    y_ref[...] = jnp.zeros(y_ref.shape, F32)
    n_chunks = base_ref[t * N_EXPERTS + N_EXPERTS - 1] + nch_ref[t * N_EXPERTS + N_EXPERTS - 1]
    for g in range(MOE_STAGE // MOE_GROUP):
        @pl.when(g * (MOE_GROUP // MOE_CHUNK) < n_chunks)
        def _():
            row = (lax.broadcasted_iota(jnp.int32, (TOKEN_TILE, MOE_GROUP), 1) + g * MOE_GROUP).astype(F32)
            rows = buf_ref[slot, g * MOE_GROUP:(g + 1) * MOE_GROUP, :]
            y_ref[...] += (w_first * _dot(jnp.where(row == first, 1.0, 0.0).astype(BF16), rows)
                           + w_second * _dot(jnp.where(row == second, 1.0, 0.0).astype(BF16), rows))

    o_ref[...] = _ln_ple(ALPHA * x_ref[...] + y_ref[...], g_ref, b_ref, p_ref, wpg_ref, wpp_ref)


def _combine(x, comb, code, plan, ys, ln_g, ln_b, p, layer, w_ple_gate, w_ple_proj):
    n = x.shape[0]
    row = lambda i, *_: (i, 0)
    const = lambda i, *_: (0, 0)
    grid_spec = pltpu.PrefetchScalarGridSpec(
        num_scalar_prefetch=3,
        grid=(n // TOKEN_TILE,),
        in_specs=[
            pl.BlockSpec((TOKEN_TILE, D_MODEL), row),
            pl.BlockSpec((TOKEN_TILE, LANES), row),
            pl.BlockSpec((TOKEN_TILE, LANES), row),
            pl.BlockSpec((8, LANES), row),
            pl.BlockSpec(memory_space=pl.ANY),
            pl.BlockSpec((1, D_MODEL), const),
            pl.BlockSpec((1, D_MODEL), const),
            pl.BlockSpec((None, TOKEN_TILE, p.shape[2]), lambda i, *_: (layer, i, 0)),
            pl.BlockSpec(w_ple_gate.shape, const),
            pl.BlockSpec(w_ple_proj.shape, const),
        ],
        out_specs=pl.BlockSpec((TOKEN_TILE, D_MODEL), row),
        scratch_shapes=[pltpu.VMEM((2, MOE_STAGE, D_MODEL), BF16), pltpu.VMEM((TOKEN_TILE, D_MODEL), F32),
                        pltpu.SemaphoreType.DMA((2,))],
    )
    return pl.pallas_call(
        _combine_kernel,
        grid_spec=grid_spec,
        out_shape=jax.ShapeDtypeStruct((n, D_MODEL), F32),
        compiler_params=_params("arbitrary"),
        name="moe_combine",
    )(plan['seg'], plan['nch'], plan['base'], x, comb, code, plan['base_rows'], ys,
      ln_g, ln_b, p, w_ple_gate, w_ple_proj)


def _moe(x, comb, code, cnt, wg, wu, wd, moe_layer, ln_g, ln_b, p, layer, w_ple_gate, w_ple_proj):
    n = x.shape[0]
    n_tiles = n // TOKEN_TILE
    tm_ffn = 1024 if n >= 8192 else 256
    max_rows = TOP_K * n + n_tiles * N_EXPERTS * (BF16_ROWS - 1) + N_EXPERTS * (MOE_CHUNK + tm_ffn - 1)
    n_ffn_tiles = -(-max_rows // tm_ffn)
    plan = _moe_plan(cnt, n_tiles, tm_ffn, n_ffn_tiles)
    xs = _dispatch(x, code, plan, n_ffn_tiles * tm_ffn)
    ys = _moe_ffn(xs, plan, wg, wu, wd, moe_layer, tm_ffn, MOE_FF_BLOCK)
    return _combine(x, comb, code, plan, ys, ln_g, ln_b, p, layer, w_ple_gate, w_ple_proj)


def _rope_tables(pos):
    inv = ROPE_THETA ** (-jnp.arange(ROPE_HALF, dtype=F32) / ROPE_HALF)
    ang = pos.astype(F32)[:, None] * inv[None, :]
    cos, sin, zero = jnp.cos(ang), jnp.sin(ang), jnp.zeros_like(ang)
    return (jnp.concatenate([cos, cos, zero, zero], axis=1),
            jnp.concatenate([-sin, sin, zero, zero], axis=1))


def _prep_mla(w_in, g_q, w_q_up, g_kv, w_kv_up, w_out):
    k_r = w_in[:, Q_LORA + KV_LORA:]
    w_in_p = jnp.concatenate([w_in[:, :Q_LORA + KV_LORA], k_r, k_r], axis=1)
    wq = w_q_up.reshape(Q_LORA, N_HEADS, QK_NOPE + QK_ROPE)
    wq_p = jnp.concatenate([wq, wq[:, :, QK_NOPE:]], axis=2).reshape(Q_LORA, N_HEADS * HEAD_PAD)
    wkv = w_kv_up.reshape(KV_LORA, N_HEADS, QK_NOPE + V_HEAD)
    w_k = wkv[:, :, :QK_NOPE].reshape(KV_LORA, -1).astype(BF16)
    w_v = wkv[:, :, QK_NOPE:].reshape(KV_LORA, -1).astype(BF16)
    return dict(w_in=w_in_p.astype(BF16), g_q=g_q[None, :], w_q_up=wq_p.astype(BF16), g_kv=g_kv[None, :],
                w_k=w_k, w_v=w_v, w_v_t=w_v.T, w_out=w_out.astype(BF16))


def _prep_gmlp_mix(w_s, b_s, chunk):
    wm = jnp.tril(w_s[:, :chunk, :chunk])
    reps = GM_MIX // chunk
    mix = jnp.einsum('ab,gts->gatbs', jnp.eye(reps, dtype=F32), wm).reshape(GM_GROUPS, GM_MIX, GM_MIX)
    bias = jnp.repeat(jnp.tile(b_s[:, :chunk].T, (reps, 1)), GM_GROUP_DIM, axis=1)
    return mix.astype(BF16), bias


def _row(v):
    return v[None, :]


def _trunk(x, p, past_latent, past_krope, W):
    b, s, _ = x.shape
    n = b * s
    tm = TOKEN_TILE
    past = 0 if past_latent is None else past_latent.shape[2]
    cos_t, sin_t = _rope_tables(past + jnp.arange(s))
    if s < tm:
        cos_t, sin_t = jnp.tile(cos_t, (tm // s, 1)), jnp.tile(sin_t, (tm // s, 1))
    x = x.reshape(n, D_MODEL)
    p = p.reshape(DEPTH, n, -1)
    chunk = min(s, GM_CHUNK)
    new_lat, new_kr, new_v = [], [], []
    for i in range(DEPTH):
        j = i // 2
        ln1 = (_row(W['ln1_g'][i]), _row(W['ln1_b'][i]))
        tail = (_row(W['ln2_g'][i]), _row(W['ln2_b'][i]), p, i, W['w_ple_gate'][i], W['w_ple_proj'][i])
        if i % 2 == 0:
            w = W['mla'][j]
            if past_latent is None:
                q, lat, kr, k, vt = _mla_proj(x, w, cos_t, sin_t, tm, True)
                att = _attn_prefill(q.reshape(b, s, -1), k.reshape(b, s, -1), vt)
                lat = lat.reshape(b, s, KV_LORA)
            else:
                q, lat, kr, krp = _mla_proj(x, w, cos_t, sin_t, tm, False)
                q = q.reshape(b, s, -1)
                lat = lat.reshape(b, s, KV_LORA)
                krp = krp.reshape(b, s, LANES)
                t_valid = past + s
                t_pad = -(-t_valid // LANES) * LANES
                lat_all = jnp.concatenate(
                    [past_latent[j], lat, jnp.zeros((b, t_pad - t_valid, KV_LORA), F32)], axis=1)
                krp_all = jnp.concatenate(
                    [jnp.pad(past_krope[j], ((0, 0), (0, 0), (0, LANES - QK_ROPE))), krp,
                     jnp.zeros((b, t_pad - t_valid, LANES), F32)], axis=1)
                k, v = _kv_up(lat_all, krp_all, w, t_pad)
                att = _attn_decode(q, k, v, past, t_valid)
            new_lat.append(lat)
            new_kr.append(kr.reshape(b, s, QK_ROPE))
            x = _attn_ffn(att.reshape(n, -1), x, w['w_out'], *ln1,
                          W['w_ffn_gate'][j], W['w_ffn_up'][j], W['w_ffn_down'][j], *tail, tm)
        else:
            w = W['gm'][j]
            mix, bias = _prep_gmlp_mix(w['w_spatial'], w['b_spatial'], chunk)
            v_rows, v_every = (GM_CHUNK, s // tm) if s >= GM_CHUNK else (tm, 1)
            x, v_new = _gmlp(x, w, mix, bias, *ln1, tm, v_rows, v_every)
            new_v.append(v_new.reshape(b, chunk, D_MODEL))
            comb, code, cnt = _router(x, W['w_router_hi'][j], W['w_router_lo'][j], tm)
            x = _moe(x, comb, code, cnt, W['w_moe_gate'], W['w_moe_up'], W['w_moe_down'], j, *tail)
    return x.reshape(b, s, D_MODEL), jnp.stack(new_lat), jnp.stack(new_kr), jnp.stack(new_v)


def kernel(x_prompt, x_sample, cache_mla_latent, cache_mla_krope, p_prompt, p_sample, w_mla_in, g_q_norm, w_q_up, g_kv_norm, w_kv_up, w_mla_out, w_gm_in, g_gm_norm, b_gm_norm, w_gm_spatial, b_gm_spatial, w_gm_out, w_ffn_gate, w_ffn_up, w_ffn_down, w_router, w_moe_gate, w_moe_up, w_moe_down, ln1_g, ln1_b, ln2_g, ln2_b, w_ple_gate, w_ple_proj):
    n_mla, n_gm = w_mla_in.shape[0], w_gm_in.shape[0]
    w_router = jnp.pad(w_router, ((0, 0), (0, 0), (0, LANES - N_EXPERTS)))
    w_router_hi = w_router.astype(BF16)
    W = dict(
        mla=[_prep_mla(w_mla_in[j], g_q_norm[j], w_q_up[j], g_kv_norm[j], w_kv_up[j], w_mla_out[j])
             for j in range(n_mla)],
        gm=[dict(w_in=w_gm_in[j].astype(BF16), g_norm=_row(g_gm_norm[j]), b_norm=_row(b_gm_norm[j]),
                 w_spatial=w_gm_spatial[j], b_spatial=b_gm_spatial[j], w_out=w_gm_out[j].astype(BF16))
            for j in range(n_gm)],
        w_ffn_gate=w_ffn_gate.astype(BF16), w_ffn_up=w_ffn_up.astype(BF16), w_ffn_down=w_ffn_down.astype(BF16),
        w_router_hi=w_router_hi, w_router_lo=(w_router - w_router_hi.astype(F32)).astype(BF16),
        w_moe_gate=w_moe_gate.astype(BF16), w_moe_up=w_moe_up.astype(BF16), w_moe_down=w_moe_down.astype(BF16),
        ln1_g=ln1_g, ln1_b=ln1_b, ln2_g=ln2_g, ln2_b=ln2_b,
        w_ple_gate=w_ple_gate.astype(BF16), w_ple_proj=w_ple_proj.astype(BF16),
    )
    y_p, lat_p, kr_p, v_p = _trunk(x_prompt, p_prompt, None, None, W)
    y_s, lat_s, kr_s, v_s = _trunk(x_sample, p_sample, cache_mla_latent, cache_mla_krope, W)
    return (y_p, y_s, lat_p, kr_p, lat_s, kr_s, v_p, v_s)
```

```python
import functools

import jax
import jax.numpy as jnp
from jax import lax
from jax.experimental import pallas as pl
from jax.experimental.pallas import tpu as pltpu

F32 = jnp.float32
BF16 = jnp.bfloat16

D_MODEL = 1024
DEPTH = 4
CHUNK = 64
N_HEADS = 8
QK_NOPE = 128
QK_ROPE = 64
V_HEAD = 128
Q_LORA = 384
KV_LORA = 256
ROPE_THETA = 10000.0
ATTN_SCALE = (QK_NOPE + QK_ROPE) ** -0.5
GM_CHUNK = 128
GM_GROUPS = 8
GM_GROUP_DIM = D_MODEL // GM_GROUPS
N_EXPERTS = 8
TOP_K = 2
ALPHA = (2 * DEPTH) ** 0.25

LANES = 128
BF16_ROWS = 16
HEAD_PAD = 2 * LANES
ROPE_HALF = QK_ROPE // 2
LOG2E = 1.4426950408889634
Q_SCALE = ATTN_SCALE * LOG2E
ATT_TILE = 256
ATT_ROWS = 32
GM_MIX = 256
FF_CHUNK = 1024
MOE_FF_BLOCK = 1792
TOKEN_TILE = 512
MOE_CHUNK = 64
MOE_SLOTS = TOP_K * TOKEN_TILE // MOE_CHUNK + N_EXPERTS
MOE_STAGE = MOE_SLOTS * MOE_CHUNK
MOE_GROUP = 256
VMEM_LIMIT = 56 * 1024 * 1024


def _params(*sem):
    return pltpu.CompilerParams(dimension_semantics=sem, vmem_limit_bytes=VMEM_LIMIT)


def _dot(a, b):
    return jnp.dot(a, b, preferred_element_type=F32)


def _layer_norm(z, g, b):
    mu = jnp.mean(z, -1, keepdims=True)
    zc = z - mu
    var = jnp.mean(zc * zc, -1, keepdims=True)
    return zc * lax.rsqrt(var + 1e-5) * g + b


def _rms_norm(z, g):
    return z * lax.rsqrt(jnp.mean(z * z, -1, keepdims=True) + 1e-6) * g


def _rope(r, cos_t, sin_t):
    return r * cos_t + pltpu.roll(r, ROPE_HALF, 1) * sin_t


def _write_kv(lat, krp, wk_ref, wv_ref, k_ref, v_ref, transpose_v):
    kn = _dot(lat, wk_ref[...])
    for hd in range(N_HEADS):
        lo = hd * HEAD_PAD
        k_ref[:, lo:lo + LANES] = kn[:, hd * LANES:(hd + 1) * LANES].astype(BF16)
        k_ref[:, lo + LANES:lo + HEAD_PAD] = krp
    if transpose_v:
        v_ref[...] = lax.dot_general(wv_ref[...], lat, (((1,), (1,)), ((), ())),
                                     preferred_element_type=F32).astype(BF16)
    else:
        v_ref[...] = _dot(lat, wv_ref[...]).astype(BF16)


def _mla_proj_kernel(*refs, with_kv):
    x_ref, win_ref, gq_ref, wq_ref, gkv_ref, cos_ref, sin_ref = refs[:7]
    if with_kv:
        wk_ref, wvt_ref, q_ref, lat_ref, kr_ref, k_ref, vt_ref = refs[7:]
    else:
        q_ref, lat_ref, kr_ref, krp_ref = refs[7:]
    xb = x_ref[...].astype(BF16)
    h = _dot(xb, win_ref[...])
    cos_t = cos_ref[...]
    sin_t = sin_ref[...]
    cq = _rms_norm(h[:, :Q_LORA], gq_ref[...])
    q = _dot(cq.astype(BF16), wq_ref[...])
    for hd in range(N_HEADS):
        lo = hd * HEAD_PAD
        q_ref[:, lo:lo + LANES] = (q[:, lo:lo + LANES] * Q_SCALE).astype(BF16)
        rot = _rope(q[:, lo + LANES:lo + HEAD_PAD], cos_t, sin_t) * Q_SCALE
        q_ref[:, lo + LANES:lo + HEAD_PAD] = rot.astype(BF16)
    lat = _rms_norm(h[:, Q_LORA:Q_LORA + KV_LORA], gkv_ref[...])
    lat_ref[...] = lat
    krot = _rope(h[:, Q_LORA + KV_LORA:], cos_t, sin_t)
    kr_ref[...] = krot[:, :QK_ROPE]
    if with_kv:
        _write_kv(lat.astype(BF16), krot.astype(BF16), wk_ref, wvt_ref, k_ref, vt_ref, True)
    else:
        krp_ref[...] = krot


def _mla_proj(x, w, cos_t, sin_t, tm, with_kv):
    n = x.shape[0]
    per_seq = cos_t.shape[0] // tm
    row = lambda i: (i, 0)
    const = lambda i: (0, 0)
    in_specs = [
        pl.BlockSpec((tm, D_MODEL), row),
        pl.BlockSpec(w['w_in'].shape, const),
        pl.BlockSpec((1, Q_LORA), const),
        pl.BlockSpec(w['w_q_up'].shape, const),
        pl.BlockSpec((1, KV_LORA), const),
        pl.BlockSpec((tm, LANES), lambda i: (i % per_seq, 0)),
        pl.BlockSpec((tm, LANES), lambda i: (i % per_seq, 0)),
    ]
    args = [x, w['w_in'], w['g_q'], w['w_q_up'], w['g_kv'], cos_t, sin_t]
    out_specs = [
        pl.BlockSpec((tm, N_HEADS * HEAD_PAD), row),
        pl.BlockSpec((tm, KV_LORA), row),
        pl.BlockSpec((tm, QK_ROPE), row),
    ]
    out_shape = [
        jax.ShapeDtypeStruct((n, N_HEADS * HEAD_PAD), BF16),
        jax.ShapeDtypeStruct((n, KV_LORA), F32),
        jax.ShapeDtypeStruct((n, QK_ROPE), F32),
    ]
    if with_kv:
        in_specs += [pl.BlockSpec(w['w_k'].shape, const), pl.BlockSpec(w['w_v_t'].shape, const)]
        args += [w['w_k'], w['w_v_t']]
        out_specs += [pl.BlockSpec((tm, N_HEADS * HEAD_PAD), row),
                      pl.BlockSpec((None, N_HEADS * V_HEAD, tm), lambda i: (i // per_seq, 0, i % per_seq))]
        out_shape += [jax.ShapeDtypeStruct((n, N_HEADS * HEAD_PAD), BF16),
                      jax.ShapeDtypeStruct((n // (per_seq * tm), N_HEADS * V_HEAD, per_seq * tm), BF16)]
    else:
        out_specs.append(pl.BlockSpec((tm, LANES), row))
        out_shape.append(jax.ShapeDtypeStruct((n, LANES), F32))
    return pl.pallas_call(
        functools.partial(_mla_proj_kernel, with_kv=with_kv),
        grid=(n // tm,),
        in_specs=in_specs,
        out_specs=out_specs,
        out_shape=out_shape,
        compiler_params=_params("parallel"),
        name="mla_proj",
    )(*args)


def _kv_up_kernel(lat_ref, krp_ref, wk_ref, wv_ref, k_ref, v_ref):
    _write_kv(lat_ref[...].astype(BF16), krp_ref[...].astype(BF16), wk_ref, wv_ref, k_ref, v_ref, False)


def _kv_up(lat, krp, w, tt):
    b, t, _ = lat.shape
    blk = lambda bi, i: (bi, i, 0)
    return pl.pallas_call(
        _kv_up_kernel,
        grid=(b, t // tt),
        in_specs=[
            pl.BlockSpec((None, tt, KV_LORA), blk),
            pl.BlockSpec((None, tt, LANES), blk),
            pl.BlockSpec(w['w_k'].shape, lambda bi, i: (0, 0)),
            pl.BlockSpec(w['w_v'].shape, lambda bi, i: (0, 0)),
        ],
        out_specs=[pl.BlockSpec((None, tt, N_HEADS * HEAD_PAD), blk), pl.BlockSpec((None, tt, N_HEADS * V_HEAD), blk)],
        out_shape=[jax.ShapeDtypeStruct((b, t, N_HEADS * HEAD_PAD), BF16),
                   jax.ShapeDtypeStruct((b, t, N_HEADS * V_HEAD), BF16)],
        compiler_params=_params("parallel", "parallel"),
        name="kv_up",
    )(lat, krp, w['w_k'], w['w_v'])


def _scores(a, b):
    return lax.dot_general(a, b, (((1,), (1,)), ((), ())), preferred_element_type=F32)


def _tree(op, parts):
    while len(parts) > 1:
        parts = [op(parts[i], parts[i + 1]) if i + 1 < len(parts) else parts[i] for i in range(0, len(parts), 2)]
    return parts[0]


def _attn_prefill_kernel(q_ref, k_ref, vt_ref, o_ref, s_ref, p_ref):
    t = ATT_TILE
    n_q = q_ref.shape[0] // t
    k_chunk = lax.broadcasted_iota(jnp.int32, (t, t), 0) // CHUNK
    q_chunk = lax.broadcasted_iota(jnp.int32, (t, t), 1) // CHUNK
    diag_visible = k_chunk <= q_chunk

    def scores(qi):
        lo, hi = qi * t, (qi + 1) * t
        s_ref[qi % 2, 0:hi, :] = _scores(k_ref[0:hi, :], q_ref[lo:hi, :])

    def softmax(qi):
        lo, slot = qi * t, qi % 2
        s = [s_ref[slot, r:r + ATT_ROWS, :] for r in range(0, lo, ATT_ROWS)] + [
            jnp.where(diag_visible[r:r + ATT_ROWS], s_ref[slot, lo + r:lo + r + ATT_ROWS, :], -jnp.inf)
            for r in range(0, t, ATT_ROWS)]
        m = jnp.max(_tree(jnp.maximum, s), axis=0, keepdims=True)
        p = [jnp.exp2(blk - m) for blk in s]
        for i, blk in enumerate(p):
            p_ref[slot, i * ATT_ROWS:(i + 1) * ATT_ROWS, :] = blk.astype(BF16)
        return jnp.sum(_tree(jnp.add, p), axis=0, keepdims=True)

    def values(qi, l):
        lo, hi = qi * t, (qi + 1) * t
        acc = _dot(vt_ref[:, 0:hi], p_ref[qi % 2, 0:hi, :])
        o_ref[lo:hi, :] = (acc / l).T.astype(o_ref.dtype)

    scores(0)
    for qi in range(n_q):
        if qi + 1 < n_q:
            scores(qi + 1)
        values(qi, softmax(qi))


def _attn_prefill(q, k, vt):
    b, s, _ = q.shape
    return pl.pallas_call(
        _attn_prefill_kernel,
        grid=(b, N_HEADS),
        in_specs=[
            pl.BlockSpec((None, s, HEAD_PAD), lambda bi, h: (bi, 0, h)),
            pl.BlockSpec((None, s, HEAD_PAD), lambda bi, h: (bi, 0, h)),
            pl.BlockSpec((None, V_HEAD, s), lambda bi, h: (bi, h, 0)),
        ],
        out_specs=pl.BlockSpec((None, s, V_HEAD), lambda bi, h: (bi, 0, h)),
        out_shape=jax.ShapeDtypeStruct((b, s, N_HEADS * V_HEAD), BF16),
        scratch_shapes=[pltpu.VMEM((2, s, ATT_TILE), F32), pltpu.VMEM((2, s, ATT_TILE), BF16)],
        compiler_params=_params("parallel", "parallel"),
        name="attn_prefill",
    )(q, k, vt)


def _attn_decode_kernel(q_ref, k_ref, v_ref, o_ref, *, q_start, t_valid):
    s_q, t = q_ref.shape[0], k_ref.shape[0]
    q_chunk = (q_start + lax.broadcasted_iota(jnp.int32, (s_q, t), 0)) // CHUNK
    k_pos = lax.broadcasted_iota(jnp.int32, (s_q, t), 1)
    visible = (k_pos // CHUNK <= q_chunk) & (k_pos < t_valid)
    for hd in range(N_HEADS):
        qk = slice(hd * HEAD_PAD, (hd + 1) * HEAD_PAD)
        vo = slice(hd * V_HEAD, (hd + 1) * V_HEAD)
        s = jnp.where(visible, _scores(q_ref[:, qk], k_ref[:, qk]), -jnp.inf)
        p = jnp.exp2(s - jnp.max(s, axis=1, keepdims=True))
        l = jnp.sum(p, axis=1, keepdims=True)
        o_ref[:, vo] = (_dot(p.astype(BF16), v_ref[:, vo]) / l).astype(o_ref.dtype)


def _attn_decode(q, k, v, q_start, t_valid):
    b, s, _ = q.shape
    t = k.shape[1]
    seq = lambda bi: (bi, 0, 0)
    return pl.pallas_call(
        functools.partial(_attn_decode_kernel, q_start=q_start, t_valid=t_valid),
        grid=(b,),
        in_specs=[
            pl.BlockSpec((None, s, N_HEADS * HEAD_PAD), seq),
            pl.BlockSpec((None, t, N_HEADS * HEAD_PAD), seq),
            pl.BlockSpec((None, t, N_HEADS * V_HEAD), seq),
        ],
        out_specs=pl.BlockSpec((None, s, N_HEADS * V_HEAD), seq),
        out_shape=jax.ShapeDtypeStruct((b, s, N_HEADS * V_HEAD), BF16),
        compiler_params=_params("parallel"),
        name="attn_decode",
    )(q, k, v)


def _router_kernel(x_ref, wh_ref, wl_ref, comb_ref, code_ref, cnt_ref):
    x = x_ref[...]
    xh = x.astype(BF16)
    xl = (x - xh.astype(F32)).astype(BF16)
    wh = wh_ref[...]
    logits = _dot(xh, wh) + (_dot(xl, wh) + _dot(xh, wl_ref[...]))
    tm = logits.shape[0]
    lane = lax.broadcasted_iota(jnp.int32, logits.shape, 1)
    logits = jnp.where(lane < N_EXPERTS, logits, -jnp.inf)
    m1 = jnp.max(logits, axis=1, keepdims=True)
    i1 = jnp.min(jnp.where(logits == m1, lane, LANES), axis=1, keepdims=True)
    rest = jnp.where(lane == i1, -jnp.inf, logits)
    m2 = jnp.max(rest, axis=1, keepdims=True)
    i2 = jnp.min(jnp.where(rest == m2, lane, LANES), axis=1, keepdims=True)
    e2 = jnp.exp(m2 - m1)
    w1 = 1.0 / (1.0 + e2)
    comb_ref[...] = jnp.where(lane == i1, w1, jnp.where(lane == i2, e2 * w1, 0.0))
    chosen = (lane == i1) | (lane == i2)
    onehot = jnp.where(chosen, 1.0, 0.0)
    earlier = lax.broadcasted_iota(jnp.int32, (tm, tm), 0) > lax.broadcasted_iota(jnp.int32, (tm, tm), 1)
    rank = _dot(jnp.where(earlier, 1.0, 0.0).astype(BF16), onehot.astype(BF16))
    code_ref[...] = jnp.where(chosen, rank, -1.0)
    cnt_ref[...] = jnp.broadcast_to(jnp.sum(onehot, axis=0, keepdims=True), cnt_ref.shape)


def _router(x, w_hi, w_lo, tm):
    n = x.shape[0]
    row = lambda i: (i, 0)
    return pl.pallas_call(
        _router_kernel,
        grid=(n // tm,),
        in_specs=[
            pl.BlockSpec((tm, D_MODEL), row),
            pl.BlockSpec(w_hi.shape, lambda i: (0, 0)),
            pl.BlockSpec(w_lo.shape, lambda i: (0, 0)),
        ],
        out_specs=[
            pl.BlockSpec((tm, LANES), row),
            pl.BlockSpec((tm, LANES), row),
            pl.BlockSpec((8, LANES), row),
        ],
        out_shape=[
            jax.ShapeDtypeStruct((n, LANES), F32),
            jax.ShapeDtypeStruct((n, LANES), F32),
            jax.ShapeDtypeStruct((n // tm * 8, LANES), F32),
        ],
        compiler_params=_params("parallel"),
        name="router",
    )(x, w_hi, w_lo)


def _gmlp_kernel(x_ref, win_ref, g_ref, b_ref, mix_ref, bias_ref, wout_ref, ln_g_ref, ln_b_ref,
                 o_ref, v_ref, *, tm, v_rows, v_every):
    x = x_ref[...]
    h = jax.nn.gelu(_dot(x.astype(BF16), win_ref[...]))
    u = h[:, :D_MODEL]
    v = _layer_norm(h[:, D_MODEL:], g_ref[...], b_ref[...])
    vb = v.astype(BF16)
    bias = bias_ref[...]
    gated = []
    for r in range(tm // GM_MIX):
        rows = slice(r * GM_MIX, (r + 1) * GM_MIX)
        cols = []
        for grp in range(GM_GROUPS):
            lanes = slice(grp * GM_GROUP_DIM, (grp + 1) * GM_GROUP_DIM)
            cols.append(_dot(mix_ref[grp], vb[rows, lanes]))
        mixed = jnp.concatenate(cols, axis=1) + bias
        gated.append((u[rows] * mixed).astype(BF16))
    y = _dot(jnp.concatenate(gated, axis=0), wout_ref[...])
    o_ref[...] = _layer_norm(ALPHA * x + y, ln_g_ref[...], ln_b_ref[...])

    @pl.when(pl.program_id(0) % v_every == v_every - 1)
    def _():
        v_ref[...] = v[tm - v_rows:, :]


def _gmlp(x, w, mix, bias, ln_g, ln_b, tm, v_rows, v_every):
    n = x.shape[0]
    row = lambda i: (i, 0)
    const = lambda i: (0, 0)
    n_v = n // (tm * v_every) * v_rows
    return pl.pallas_call(
        functools.partial(_gmlp_kernel, tm=tm, v_rows=v_rows, v_every=v_every),
        grid=(n // tm,),
        in_specs=[
            pl.BlockSpec((tm, D_MODEL), row),
            pl.BlockSpec(w['w_in'].shape, const),
            pl.BlockSpec((1, D_MODEL), const),
            pl.BlockSpec((1, D_MODEL), const),
            pl.BlockSpec(mix.shape, lambda i: (0, 0, 0)),
            pl.BlockSpec(bias.shape, const),
            pl.BlockSpec(w['w_out'].shape, const),
            pl.BlockSpec((1, D_MODEL), const),
            pl.BlockSpec((1, D_MODEL), const),
        ],
        out_specs=[
            pl.BlockSpec((tm, D_MODEL), row),
            pl.BlockSpec((v_rows, D_MODEL), lambda i: (i // v_every, 0)),
        ],
        out_shape=[
            jax.ShapeDtypeStruct((n, D_MODEL), F32),
            jax.ShapeDtypeStruct((n_v, D_MODEL), F32),
        ],
        compiler_params=_params("arbitrary"),
        name="gmlp",
    )(x, w['w_in'], w['g_norm'], w['b_norm'], mix, bias, w['w_out'], ln_g, ln_b)


def _ln_ple(z, g_ref, b_ref, p_ref, wpg_ref, wpp_ref):
    x2 = _layer_norm(z, g_ref[...], b_ref[...])
    gate = jax.nn.sigmoid(_dot(x2.astype(BF16), wpg_ref[...]))
    return x2 + gate * _dot(p_ref[...].astype(BF16), wpp_ref[...])


def _swiglu(xb, wg_ref, wu_ref, wd_ref):
    d_ff = wg_ref.shape[1]
    y = None
    for lo in range(0, d_ff, FF_CHUNK):
        hi = min(lo + FF_CHUNK, d_ff)
        h = jax.nn.silu(_dot(xb, wg_ref[:, lo:hi])) * _dot(xb, wu_ref[:, lo:hi])
        d = _dot(h.astype(BF16), wd_ref[lo:hi, :])
        y = d if y is None else y + d
    return y


def _attn_ffn_kernel(a_ref, x_ref, wo_ref, g1_ref, b1_ref, wg_ref, wu_ref, wd_ref, g_ref, b_ref, p_ref,
                     wpg_ref, wpp_ref, o_ref):
    x1 = _layer_norm(ALPHA * x_ref[...] + _dot(a_ref[...], wo_ref[...]), g1_ref[...], b1_ref[...])
    y = _swiglu(x1.astype(BF16), wg_ref, wu_ref, wd_ref)
    o_ref[...] = _ln_ple(ALPHA * x1 + y, g_ref, b_ref, p_ref, wpg_ref, wpp_ref)


def _resident(shape):
    return pl.BlockSpec(shape, lambda i: (0,) * len(shape), pipeline_mode=pl.Buffered(1))


def _attn_ffn(a, x, w_out, ln1_g, ln1_b, wg, wu, wd, ln_g, ln_b, p, layer, w_ple_gate, w_ple_proj, tm):
    n = x.shape[0]
    row = lambda i: (i, 0)
    const = lambda i: (0, 0)
    return pl.pallas_call(
        _attn_ffn_kernel,
        grid=(n // tm,),
        in_specs=[
            pl.BlockSpec((tm, a.shape[1]), row),
            pl.BlockSpec((tm, D_MODEL), row),
            _resident(w_out.shape),
            pl.BlockSpec((1, D_MODEL), const),
            pl.BlockSpec((1, D_MODEL), const),
            _resident(wg.shape),
            _resident(wu.shape),
            _resident(wd.shape),
            pl.BlockSpec((1, D_MODEL), const),
            pl.BlockSpec((1, D_MODEL), const),
            pl.BlockSpec((None, tm, p.shape[2]), lambda i: (layer, i, 0)),
            _resident(w_ple_gate.shape),
            _resident(w_ple_proj.shape),
        ],
        out_specs=pl.BlockSpec((tm, D_MODEL), row),
        out_shape=jax.ShapeDtypeStruct((n, D_MODEL), F32),
        compiler_params=_params("parallel"),
        name="attn_ffn",
    )(a, x, w_out, ln1_g, ln1_b, wg, wu, wd, ln_g, ln_b, p, w_ple_gate, w_ple_proj)


def _moe_plan(cnt, n_tiles, tm_ffn, n_ffn_tiles):
    cnt = cnt.reshape(n_tiles, 8, LANES)[:, 0, :N_EXPERTS].astype(jnp.int32)
    nch = (cnt + MOE_CHUNK - 1) // MOE_CHUNK
    base = jnp.cumsum(nch, axis=1) - nch
    seg = (cnt + BF16_ROWS - 1) // BF16_ROWS * BF16_ROWS
    content = jnp.sum(seg, axis=0)
    group_pad = (content + MOE_CHUNK + tm_ffn - 1) // tm_ffn * tm_ffn
    group_start = jnp.cumsum(group_pad) - group_pad
    seg_start = group_start[None, :] + jnp.cumsum(seg, axis=0) - seg
    k = jnp.arange(MOE_SLOTS)[None, :, None]
    lo, hi = base[:, None, :], (base + nch)[:, None, :]
    slot_dst = jnp.sum(jnp.where((k >= lo) & (k < hi), seg_start[:, None, :] + (k - lo) * MOE_CHUNK, 0), axis=2)
    tile_end = jnp.cumsum(group_pad // tm_ffn)
    n_used = tile_end[-1]
    tile_id = jnp.minimum(jnp.arange(n_ffn_tiles), n_used - 1)
    tile_expert = jnp.sum(tile_id[:, None] >= tile_end[None, :], axis=1)
    base_rows = jnp.zeros((n_tiles, 8, LANES), F32).at[:, :, :N_EXPERTS].set(
        (base * MOE_CHUNK).astype(F32)[:, None, :]).reshape(n_tiles * 8, LANES)
    flat = lambda a: a.reshape(-1).astype(jnp.int32)
    used_rows = n_used * tm_ffn
    zero_row = jnp.concatenate([group_start + content, used_rows[None]])
    zero_cnt = jnp.concatenate([group_pad - content, (n_ffn_tiles * tm_ffn - used_rows)[None]]) // BF16_ROWS
    return dict(slot_dst=flat(slot_dst), n_chunks=flat(jnp.sum(nch, axis=1)), base_rows=base_rows,
                zero_row=flat(zero_row), zero_cnt=flat(zero_cnt),
                tile_expert=flat(tile_expert), n_used=flat(n_used))


def _staged_rows(code, base_rows):
    pos = jnp.where(code >= 0.0, base_rows + code, -1.0)
    first = jnp.max(pos, axis=1, keepdims=True)
    second = jnp.max(jnp.where(pos == first, -1.0, pos), axis=1, keepdims=True)
    return pos, first, second


def _chunk_copies(hbm_ref, vmem_ref, sem, dst_ref, nck_ref, tile, slot, to_hbm, action):
    def body(k, carry):
        stage = vmem_ref.at[slot, pl.ds(pl.multiple_of(k * MOE_CHUNK, MOE_CHUNK), MOE_CHUNK)]
        group = hbm_ref.at[pl.ds(pl.multiple_of(dst_ref[tile * MOE_SLOTS + k], BF16_ROWS), MOE_CHUNK)]
        src, dst = (stage, group) if to_hbm else (group, stage)
        copy = pltpu.make_async_copy(src, dst, sem.at[slot])
        copy.start() if action == "start" else copy.wait()
        return carry

    lax.fori_loop(0, nck_ref[tile], body, 0)


def _zero_fill(xs_ref, zero_ref, sem, zrow_ref, zcnt_ref, action):
    for e in range(N_EXPERTS + 1):
        def body(k, carry):
            rows = xs_ref.at[pl.ds(pl.multiple_of(zrow_ref[e] + k * BF16_ROWS, BF16_ROWS), BF16_ROWS)]
            copy = pltpu.make_async_copy(zero_ref, rows, sem.at[2])
            copy.start() if action == "start" else copy.wait()
            return carry

        lax.fori_loop(0, zcnt_ref[e], body, 0)


def _dispatch_kernel(dst_ref, nck_ref, zrow_ref, zcnt_ref, x_ref, code_ref, brow_ref, xs_ref,
                     stage_ref, zero_ref, sem):
    t = pl.program_id(0)
    slot = t % 2
    _, first, second = _staged_rows(code_ref[...], brow_ref[0:1, :])
    lane = lax.broadcasted_iota(jnp.int32, (TOKEN_TILE, LANES), 1)
    rows_t = jnp.where(lane == 0, first, jnp.where(lane == 1, second, -1.0)).T
    r1, r2 = rows_t[0:1, :], rows_t[1:2, :]
    xb = x_ref[...].astype(BF16)
    n_chunks = nck_ref[t]
    for g in range(MOE_STAGE // MOE_GROUP):
        @pl.when(g * (MOE_GROUP // MOE_CHUNK) < n_chunks)
        def _():
            row = (lax.broadcasted_iota(jnp.int32, (MOE_GROUP, TOKEN_TILE), 0) + g * MOE_GROUP).astype(F32)
            sel = jnp.where((row == r1) | (row == r2), 1.0, 0.0).astype(BF16)
            stage_ref[slot, g * MOE_GROUP:(g + 1) * MOE_GROUP, :] = _dot(sel, xb).astype(BF16)

    copies = functools.partial(_chunk_copies, xs_ref, stage_ref, sem, dst_ref, nck_ref, to_hbm=True)

    @pl.when(t > 0)
    def _():
        copies(tile=t - 1, slot=1 - slot, action="wait")

    copies(tile=t, slot=slot, action="start")

    @pl.when(t == pl.num_programs(0) - 1)
    def _():
        copies(tile=t, slot=slot, action="wait")
        zero_ref[...] = jnp.zeros(zero_ref.shape, BF16)
        _zero_fill(xs_ref, zero_ref, sem, zrow_ref, zcnt_ref, "start")
        _zero_fill(xs_ref, zero_ref, sem, zrow_ref, zcnt_ref, "wait")


def _dispatch(x, code, plan, n_rows):
    n = x.shape[0]
    row = lambda i, *_: (i, 0)
    grid_spec = pltpu.PrefetchScalarGridSpec(
        num_scalar_prefetch=4,
        grid=(n // TOKEN_TILE,),
        in_specs=[
            pl.BlockSpec((TOKEN_TILE, D_MODEL), row),
            pl.BlockSpec((TOKEN_TILE, LANES), row),
            pl.BlockSpec((8, LANES), row),
        ],
        out_specs=pl.BlockSpec(memory_space=pl.ANY),
        scratch_shapes=[pltpu.VMEM((2, MOE_STAGE, D_MODEL), BF16), pltpu.VMEM((BF16_ROWS, D_MODEL), BF16),
                        pltpu.SemaphoreType.DMA((3,))],
    )
    return pl.pallas_call(
        _dispatch_kernel,
        grid_spec=grid_spec,
        out_shape=jax.ShapeDtypeStruct((n_rows, D_MODEL), BF16),
        compiler_params=_params("arbitrary"),
        name="moe_dispatch",
    )(plan['slot_dst'], plan['n_chunks'], plan['zero_row'], plan['zero_cnt'], x, code, plan['base_rows'])


def _moe_ffn_kernel(te_ref, nu_ref, x_ref, wg_ref, wu_ref, wd_ref, o_ref, acc_ref):
    del te_ref
    f = pl.program_id(1)
    used = pl.program_id(0) < nu_ref[0]

    @pl.when(jnp.logical_not(used) & (f == 0))
    def _():
        o_ref[...] = jnp.zeros(o_ref.shape, o_ref.dtype)

    @pl.when(used)
    def _():
        @pl.when(f == 0)
        def _():
            acc_ref[...] = jnp.zeros(acc_ref.shape, F32)

        acc_ref[...] += _swiglu(x_ref[...], wg_ref, wu_ref, wd_ref)

        @pl.when(f == pl.num_programs(1) - 1)
        def _():
            o_ref[...] = acc_ref[...].astype(o_ref.dtype)


def _moe_ffn(xs, plan, wg, wu, wd, layer, tm, tf):
    n_rows = xs.shape[0]
    n_f = wg.shape[-1] // tf
    tile = lambda i, f, te, nu: (jnp.minimum(i, nu[0] - 1), 0)
    f_idx = lambda i, f, nu: jnp.where(i < nu[0], f, n_f - 1)
    grid_spec = pltpu.PrefetchScalarGridSpec(
        num_scalar_prefetch=2,
        grid=(n_rows // tm, n_f),
        in_specs=[
            pl.BlockSpec((tm, D_MODEL), tile),
            pl.BlockSpec((None, None, D_MODEL, tf), lambda i, f, te, nu: (layer, te[i], 0, f_idx(i, f, nu))),
            pl.BlockSpec((None, None, D_MODEL, tf), lambda i, f, te, nu: (layer, te[i], 0, f_idx(i, f, nu))),
            pl.BlockSpec((None, None, tf, D_MODEL), lambda i, f, te, nu: (layer, te[i], f_idx(i, f, nu), 0)),
        ],
        out_specs=pl.BlockSpec((tm, D_MODEL), lambda i, f, te, nu: (i, 0)),
        scratch_shapes=[pltpu.VMEM((tm, D_MODEL), F32)],
    )
    return pl.pallas_call(
        _moe_ffn_kernel,
        grid_spec=grid_spec,
        out_shape=jax.ShapeDtypeStruct((n_rows, D_MODEL), BF16),
        compiler_params=_params("arbitrary", "arbitrary"),
        name="moe_ffn",
    )(plan['tile_expert'], plan['n_used'], xs, wg, wu, wd)


def _combine_kernel(dst_ref, nck_ref, x_ref, comb_ref, code_ref, brow_ref, ys_ref,
                    g_ref, b_ref, p_ref, wpg_ref, wpp_ref, o_ref, buf_ref, y_ref, sem):
    t = pl.program_id(0)
    slot = t % 2
    copies = functools.partial(_chunk_copies, ys_ref, buf_ref, sem, dst_ref, nck_ref, to_hbm=False)

    @pl.when(t == 0)
    def _():
        buf_ref[...] = jnp.zeros(buf_ref.shape, BF16)
        copies(tile=t, slot=slot, action="start")

    @pl.when(t + 1 < pl.num_programs(0))
    def _():
        copies(tile=t + 1, slot=1 - slot, action="start")

    copies(tile=t, slot=slot, action="wait")

    comb = comb_ref[...]
    pos, first, second = _staged_rows(code_ref[...], brow_ref[0:1, :])
    w_first = jnp.max(jnp.where(pos == first, comb, 0.0), axis=1, keepdims=True)
    w_second = jnp.max(jnp.where(pos == second, comb, 0.0), axis=1, keepdims=True)
    y_ref[...] = jnp.zeros(y_ref.shape, F32)
    n_chunks = nck_ref[t]
    for g in range(MOE_STAGE // MOE_GROUP):
        @pl.when(g * (MOE_GROUP // MOE_CHUNK) < n_chunks)
        def _():
            row = (lax.broadcasted_iota(jnp.int32, (TOKEN_TILE, MOE_GROUP), 1) + g * MOE_GROUP).astype(F32)
            rows = buf_ref[slot, g * MOE_GROUP:(g + 1) * MOE_GROUP, :]
            y_ref[...] += (w_first * _dot(jnp.where(row == first, 1.0, 0.0).astype(BF16), rows)
                           + w_second * _dot(jnp.where(row == second, 1.0, 0.0).astype(BF16), rows))

    o_ref[...] = _ln_ple(ALPHA * x_ref[...] + y_ref[...], g_ref, b_ref, p_ref, wpg_ref, wpp_ref)


def _combine(x, comb, code, plan, ys, ln_g, ln_b, p, layer, w_ple_gate, w_ple_proj):
    n = x.shape[0]
    row = lambda i, *_: (i, 0)
    const = lambda i, *_: (0, 0)
    grid_spec = pltpu.PrefetchScalarGridSpec(
        num_scalar_prefetch=2,
        grid=(n // TOKEN_TILE,),
        in_specs=[
            pl.BlockSpec((TOKEN_TILE, D_MODEL), row),
            pl.BlockSpec((TOKEN_TILE, LANES), row),
            pl.BlockSpec((TOKEN_TILE, LANES), row),
            pl.BlockSpec((8, LANES), row),
            pl.BlockSpec(memory_space=pl.ANY),
            pl.BlockSpec((1, D_MODEL), const),
            pl.BlockSpec((1, D_MODEL), const),
            pl.BlockSpec((None, TOKEN_TILE, p.shape[2]), lambda i, *_: (layer, i, 0)),
            pl.BlockSpec(w_ple_gate.shape, const),
            pl.BlockSpec(w_ple_proj.shape, const),
        ],
        out_specs=pl.BlockSpec((TOKEN_TILE, D_MODEL), row),
        scratch_shapes=[pltpu.VMEM((2, MOE_STAGE, D_MODEL), BF16), pltpu.VMEM((TOKEN_TILE, D_MODEL), F32),
                        pltpu.SemaphoreType.DMA((2,))],
    )
    return pl.pallas_call(
        _combine_kernel,
        grid_spec=grid_spec,
        out_shape=jax.ShapeDtypeStruct((n, D_MODEL), F32),
        compiler_params=_params("arbitrary"),
        name="moe_combine",
    )(plan['slot_dst'], plan['n_chunks'], x, comb, code, plan['base_rows'], ys,
      ln_g, ln_b, p, w_ple_gate, w_ple_proj)


def _moe(x, comb, code, cnt, wg, wu, wd, moe_layer, ln_g, ln_b, p, layer, w_ple_gate, w_ple_proj):
    n = x.shape[0]
    n_tiles = n // TOKEN_TILE
    tm_ffn = 1024 if n >= 8192 else 256
    max_rows = TOP_K * n + n_tiles * N_EXPERTS * (BF16_ROWS - 1) + N_EXPERTS * (MOE_CHUNK + tm_ffn - 1)
    n_ffn_tiles = -(-max_rows // tm_ffn)
    plan = _moe_plan(cnt, n_tiles, tm_ffn, n_ffn_tiles)
    xs = _dispatch(x, code, plan, n_ffn_tiles * tm_ffn)
    ys = _moe_ffn(xs, plan, wg, wu, wd, moe_layer, tm_ffn, MOE_FF_BLOCK)
    return _combine(x, comb, code, plan, ys, ln_g, ln_b, p, layer, w_ple_gate, w_ple_proj)


def _rope_tables(pos):
    inv = ROPE_THETA ** (-jnp.arange(ROPE_HALF, dtype=F32) / ROPE_HALF)
    ang = pos.astype(F32)[:, None] * inv[None, :]
    cos, sin, zero = jnp.cos(ang), jnp.sin(ang), jnp.zeros_like(ang)
    return (jnp.concatenate([cos, cos, zero, zero], axis=1),
            jnp.concatenate([-sin, sin, zero, zero], axis=1))


def _prep_mla(w_in, g_q, w_q_up, g_kv, w_kv_up, w_out):
    k_r = w_in[:, Q_LORA + KV_LORA:]
    w_in_p = jnp.concatenate([w_in[:, :Q_LORA + KV_LORA], k_r, k_r], axis=1)
    wq = w_q_up.reshape(Q_LORA, N_HEADS, QK_NOPE + QK_ROPE)
    wq_p = jnp.concatenate([wq, wq[:, :, QK_NOPE:]], axis=2).reshape(Q_LORA, N_HEADS * HEAD_PAD)
    wkv = w_kv_up.reshape(KV_LORA, N_HEADS, QK_NOPE + V_HEAD)
    w_k = wkv[:, :, :QK_NOPE].reshape(KV_LORA, -1).astype(BF16)
    w_v = wkv[:, :, QK_NOPE:].reshape(KV_LORA, -1).astype(BF16)
    return dict(w_in=w_in_p.astype(BF16), g_q=g_q[None, :], w_q_up=wq_p.astype(BF16), g_kv=g_kv[None, :],
                w_k=w_k, w_v=w_v, w_v_t=w_v.T, w_out=w_out.astype(BF16))


def _prep_gmlp_mix(w_s, b_s, chunk):
    wm = jnp.tril(w_s[:, :chunk, :chunk])
    reps = GM_MIX // chunk
    mix = jnp.einsum('ab,gts->gatbs', jnp.eye(reps, dtype=F32), wm).reshape(GM_GROUPS, GM_MIX, GM_MIX)
    bias = jnp.repeat(jnp.tile(b_s[:, :chunk].T, (reps, 1)), GM_GROUP_DIM, axis=1)
    return mix.astype(BF16), bias


def _row(v):
    return v[None, :]


def _trunk(x, p, past_latent, past_krope, W):
    b, s, _ = x.shape
    n = b * s
    tm = TOKEN_TILE
    past = 0 if past_latent is None else past_latent.shape[2]
    cos_t, sin_t = _rope_tables(past + jnp.arange(s))
    if s < tm:
        cos_t, sin_t = jnp.tile(cos_t, (tm // s, 1)), jnp.tile(sin_t, (tm // s, 1))
    x = x.reshape(n, D_MODEL)
    p = p.reshape(DEPTH, n, -1)
    chunk = min(s, GM_CHUNK)
    new_lat, new_kr, new_v = [], [], []
    for i in range(DEPTH):
        j = i // 2
        ln1 = (_row(W['ln1_g'][i]), _row(W['ln1_b'][i]))
        tail = (_row(W['ln2_g'][i]), _row(W['ln2_b'][i]), p, i, W['w_ple_gate'][i], W['w_ple_proj'][i])
        if i % 2 == 0:
            w = W['mla'][j]
            if past_latent is None:
                q, lat, kr, k, vt = _mla_proj(x, w, cos_t, sin_t, tm, True)
                att = _attn_prefill(q.reshape(b, s, -1), k.reshape(b, s, -1), vt)
                lat = lat.reshape(b, s, KV_LORA)
            else:
                q, lat, kr, krp = _mla_proj(x, w, cos_t, sin_t, tm, False)
                q = q.reshape(b, s, -1)
                lat = lat.reshape(b, s, KV_LORA)
                krp = krp.reshape(b, s, LANES)
                t_valid = past + s
                t_pad = -(-t_valid // LANES) * LANES
                lat_all = jnp.concatenate(
                    [past_latent[j], lat, jnp.zeros((b, t_pad - t_valid, KV_LORA), F32)], axis=1)
                krp_all = jnp.concatenate(
                    [jnp.pad(past_krope[j], ((0, 0), (0, 0), (0, LANES - QK_ROPE))), krp,
                     jnp.zeros((b, t_pad - t_valid, LANES), F32)], axis=1)
                k, v = _kv_up(lat_all, krp_all, w, t_pad)
                att = _attn_decode(q, k, v, past, t_valid)
            new_lat.append(lat)
            new_kr.append(kr.reshape(b, s, QK_ROPE))
            x = _attn_ffn(att.reshape(n, -1), x, w['w_out'], *ln1,
                          W['w_ffn_gate'][j], W['w_ffn_up'][j], W['w_ffn_down'][j], *tail, tm)
        else:
            w = W['gm'][j]
            mix, bias = _prep_gmlp_mix(w['w_spatial'], w['b_spatial'], chunk)
            v_rows, v_every = (GM_CHUNK, s // tm) if s >= GM_CHUNK else (tm, 1)
            x, v_new = _gmlp(x, w, mix, bias, *ln1, tm, v_rows, v_every)
            new_v.append(v_new.reshape(b, chunk, D_MODEL))
            comb, code, cnt = _router(x, W['w_router_hi'][j], W['w_router_lo'][j], tm)
            x = _moe(x, comb, code, cnt, W['w_moe_gate'], W['w_moe_up'], W['w_moe_down'], j, *tail)
    return x.reshape(b, s, D_MODEL), jnp.stack(new_lat), jnp.stack(new_kr), jnp.stack(new_v)


def kernel(x_prompt, x_sample, cache_mla_latent, cache_mla_krope, p_prompt, p_sample, w_mla_in, g_q_norm, w_q_up, g_kv_norm, w_kv_up, w_mla_out, w_gm_in, g_gm_norm, b_gm_norm, w_gm_spatial, b_gm_spatial, w_gm_out, w_ffn_gate, w_ffn_up, w_ffn_down, w_router, w_moe_gate, w_moe_up, w_moe_down, ln1_g, ln1_b, ln2_g, ln2_b, w_ple_gate, w_ple_proj):
    n_mla, n_gm = w_mla_in.shape[0], w_gm_in.shape[0]
    w_router = jnp.pad(w_router, ((0, 0), (0, 0), (0, LANES - N_EXPERTS)))
    w_router_hi = w_router.astype(BF16)
    W = dict(
        mla=[_prep_mla(w_mla_in[j], g_q_norm[j], w_q_up[j], g_kv_norm[j], w_kv_up[j], w_mla_out[j])
             for j in range(n_mla)],
        gm=[dict(w_in=w_gm_in[j].astype(BF16), g_norm=_row(g_gm_norm[j]), b_norm=_row(b_gm_norm[j]),
                 w_spatial=w_gm_spatial[j], b_spatial=b_gm_spatial[j], w_out=w_gm_out[j].astype(BF16))
            for j in range(n_gm)],
        w_ffn_gate=w_ffn_gate.astype(BF16), w_ffn_up=w_ffn_up.astype(BF16), w_ffn_down=w_ffn_down.astype(BF16),
        w_router_hi=w_router_hi, w_router_lo=(w_router - w_router_hi.astype(F32)).astype(BF16),
        w_moe_gate=w_moe_gate.astype(BF16), w_moe_up=w_moe_up.astype(BF16), w_moe_down=w_moe_down.astype(BF16),
        ln1_g=ln1_g, ln1_b=ln1_b, ln2_g=ln2_g, ln2_b=ln2_b,
        w_ple_gate=w_ple_gate.astype(BF16), w_ple_proj=w_ple_proj.astype(BF16),
    )
    y_p, lat_p, kr_p, v_p = _trunk(x_prompt, p_prompt, None, None, W)
    y_s, lat_s, kr_s, v_s = _trunk(x_sample, p_sample, cache_mla_latent, cache_mla_krope, W)
    return (y_p, y_s, lat_p, kr_p, lat_s, kr_s, v_p, v_s)
```

```python
import functools

import jax
import jax.numpy as jnp
from jax import lax
from jax.experimental import pallas as pl
from jax.experimental.pallas import tpu as pltpu

F32 = jnp.float32
BF16 = jnp.bfloat16

D_MODEL = 1024
DEPTH = 4
CHUNK = 64
N_HEADS = 8
QK_NOPE = 128
QK_ROPE = 64
V_HEAD = 128
Q_LORA = 384
KV_LORA = 256
ROPE_THETA = 10000.0
ATTN_SCALE = (QK_NOPE + QK_ROPE) ** -0.5
GM_CHUNK = 128
GM_GROUPS = 8
GM_GROUP_DIM = D_MODEL // GM_GROUPS
N_EXPERTS = 8
TOP_K = 2
ALPHA = (2 * DEPTH) ** 0.25

LANES = 128
BF16_ROWS = 16
HEAD_PAD = 2 * LANES
ROPE_HALF = QK_ROPE // 2
LOG2E = 1.4426950408889634
Q_SCALE = ATTN_SCALE * LOG2E
ATT_TILE = 256
ATT_ROWS = 32
ATT_AHEAD = 3
ATT_SLOTS = ATT_AHEAD + 1
GM_MIX = 256
FF_CHUNK = 1024
MOE_FF_BLOCK = 1792
TOKEN_TILE = 512
MOE_CHUNK = 64
MOE_SLOTS = TOP_K * TOKEN_TILE // MOE_CHUNK + N_EXPERTS
MOE_STAGE = MOE_SLOTS * MOE_CHUNK
MOE_GROUP = 256
VMEM_LIMIT = 56 * 1024 * 1024


def _params(*sem):
    return pltpu.CompilerParams(dimension_semantics=sem, vmem_limit_bytes=VMEM_LIMIT)


def _dot(a, b):
    return jnp.dot(a, b, preferred_element_type=F32)


def _layer_norm(z, g, b):
    mu = jnp.mean(z, -1, keepdims=True)
    zc = z - mu
    var = jnp.mean(zc * zc, -1, keepdims=True)
    return zc * lax.rsqrt(var + 1e-5) * g + b


def _rms_norm(z, g):
    return z * lax.rsqrt(jnp.mean(z * z, -1, keepdims=True) + 1e-6) * g


def _rope(r, cos_t, sin_t):
    return r * cos_t + pltpu.roll(r, ROPE_HALF, 1) * sin_t


def _write_kv(lat, krp, wk_ref, wv_ref, k_ref, v_ref, transpose_v):
    kn = _dot(lat, wk_ref[...])
    for hd in range(N_HEADS):
        lo = hd * HEAD_PAD
        k_ref[:, lo:lo + LANES] = kn[:, hd * LANES:(hd + 1) * LANES].astype(BF16)
        k_ref[:, lo + LANES:lo + HEAD_PAD] = krp
    if transpose_v:
        v_ref[...] = lax.dot_general(wv_ref[...], lat, (((1,), (1,)), ((), ())),
                                     preferred_element_type=F32).astype(BF16)
    else:
        v_ref[...] = _dot(lat, wv_ref[...]).astype(BF16)


def _mla_proj_kernel(*refs, with_kv):
    x_ref, win_ref, gq_ref, wq_ref, gkv_ref, cos_ref, sin_ref = refs[:7]
    if with_kv:
        wk_ref, wvt_ref, q_ref, lat_ref, kr_ref, k_ref, vt_ref = refs[7:]
    else:
        q_ref, lat_ref, kr_ref, krp_ref = refs[7:]
    xb = x_ref[...].astype(BF16)
    h = _dot(xb, win_ref[...])
    cos_t = cos_ref[...]
    sin_t = sin_ref[...]
    cq = _rms_norm(h[:, :Q_LORA], gq_ref[...])
    q = _dot(cq.astype(BF16), wq_ref[...])
    for hd in range(N_HEADS):
        lo = hd * HEAD_PAD
        q_ref[:, lo:lo + LANES] = (q[:, lo:lo + LANES] * Q_SCALE).astype(BF16)
        rot = _rope(q[:, lo + LANES:lo + HEAD_PAD], cos_t, sin_t) * Q_SCALE
        q_ref[:, lo + LANES:lo + HEAD_PAD] = rot.astype(BF16)
    lat = _rms_norm(h[:, Q_LORA:Q_LORA + KV_LORA], gkv_ref[...])
    lat_ref[...] = lat
    krot = _rope(h[:, Q_LORA + KV_LORA:], cos_t, sin_t)
    kr_ref[...] = krot[:, :QK_ROPE]
    if with_kv:
        _write_kv(lat.astype(BF16), krot.astype(BF16), wk_ref, wvt_ref, k_ref, vt_ref, True)
    else:
        krp_ref[...] = krot


def _mla_proj(x, w, cos_t, sin_t, tm, with_kv):
    n = x.shape[0]
    per_seq = cos_t.shape[0] // tm
    row = lambda i: (i, 0)
    const = lambda i: (0, 0)
    in_specs = [
        pl.BlockSpec((tm, D_MODEL), row),
        pl.BlockSpec(w['w_in'].shape, const),
        pl.BlockSpec((1, Q_LORA), const),
        pl.BlockSpec(w['w_q_up'].shape, const),
        pl.BlockSpec((1, KV_LORA), const),
        pl.BlockSpec((tm, LANES), lambda i: (i % per_seq, 0)),
        pl.BlockSpec((tm, LANES), lambda i: (i % per_seq, 0)),
    ]
    args = [x, w['w_in'], w['g_q'], w['w_q_up'], w['g_kv'], cos_t, sin_t]
    out_specs = [
        pl.BlockSpec((tm, N_HEADS * HEAD_PAD), row),
        pl.BlockSpec((tm, KV_LORA), row),
        pl.BlockSpec((tm, QK_ROPE), row),
    ]
    out_shape = [
        jax.ShapeDtypeStruct((n, N_HEADS * HEAD_PAD), BF16),
        jax.ShapeDtypeStruct((n, KV_LORA), F32),
        jax.ShapeDtypeStruct((n, QK_ROPE), F32),
    ]
    if with_kv:
        in_specs += [pl.BlockSpec(w['w_k'].shape, const), pl.BlockSpec(w['w_v_t'].shape, const)]
        args += [w['w_k'], w['w_v_t']]
        out_specs += [pl.BlockSpec((tm, N_HEADS * HEAD_PAD), row),
                      pl.BlockSpec((None, N_HEADS * V_HEAD, tm), lambda i: (i // per_seq, 0, i % per_seq))]
        out_shape += [jax.ShapeDtypeStruct((n, N_HEADS * HEAD_PAD), BF16),
                      jax.ShapeDtypeStruct((n // (per_seq * tm), N_HEADS * V_HEAD, per_seq * tm), BF16)]
    else:
        out_specs.append(pl.BlockSpec((tm, LANES), row))
        out_shape.append(jax.ShapeDtypeStruct((n, LANES), F32))
    return pl.pallas_call(
        functools.partial(_mla_proj_kernel, with_kv=with_kv),
        grid=(n // tm,),
        in_specs=in_specs,
        out_specs=out_specs,
        out_shape=out_shape,
        compiler_params=_params("parallel"),
        name="mla_proj",
    )(*args)


def _kv_up_kernel(lat_ref, krp_ref, wk_ref, wv_ref, k_ref, v_ref):
    _write_kv(lat_ref[...].astype(BF16), krp_ref[...].astype(BF16), wk_ref, wv_ref, k_ref, v_ref, False)


def _kv_up(lat, krp, w, tt):
    b, t, _ = lat.shape
    blk = lambda bi, i: (bi, i, 0)
    return pl.pallas_call(
        _kv_up_kernel,
        grid=(b, t // tt),
        in_specs=[
            pl.BlockSpec((None, tt, KV_LORA), blk),
            pl.BlockSpec((None, tt, LANES), blk),
            pl.BlockSpec(w['w_k'].shape, lambda bi, i: (0, 0)),
            pl.BlockSpec(w['w_v'].shape, lambda bi, i: (0, 0)),
        ],
        out_specs=[pl.BlockSpec((None, tt, N_HEADS * HEAD_PAD), blk), pl.BlockSpec((None, tt, N_HEADS * V_HEAD), blk)],
        out_shape=[jax.ShapeDtypeStruct((b, t, N_HEADS * HEAD_PAD), BF16),
                   jax.ShapeDtypeStruct((b, t, N_HEADS * V_HEAD), BF16)],
        compiler_params=_params("parallel", "parallel"),
        name="kv_up",
    )(lat, krp, w['w_k'], w['w_v'])


def _scores(a, b):
    return lax.dot_general(a, b, (((1,), (1,)), ((), ())), preferred_element_type=F32)


def _tree(op, parts):
    while len(parts) > 1:
        parts = [op(parts[i], parts[i + 1]) if i + 1 < len(parts) else parts[i] for i in range(0, len(parts), 2)]
    return parts[0]


def _attn_prefill_kernel(q_ref, k_ref, vt_ref, o_ref, s_ref, p_ref):
    t = ATT_TILE
    n_q = q_ref.shape[0] // t
    k_chunk = lax.broadcasted_iota(jnp.int32, (t, t), 0) // CHUNK
    q_chunk = lax.broadcasted_iota(jnp.int32, (t, t), 1) // CHUNK
    diag_visible = k_chunk <= q_chunk

    def scores(qi):
        lo, hi = qi * t, (qi + 1) * t
        s_ref[qi % ATT_SLOTS, 0:hi, :] = _scores(k_ref[0:hi, :], q_ref[lo:hi, :])

    def softmax(qi):
        lo, slot = qi * t, qi % ATT_SLOTS
        s = [s_ref[slot, r:r + ATT_ROWS, :] for r in range(0, lo, ATT_ROWS)] + [
            jnp.where(diag_visible[r:r + ATT_ROWS], s_ref[slot, lo + r:lo + r + ATT_ROWS, :], -jnp.inf)
            for r in range(0, t, ATT_ROWS)]
        m = jnp.max(_tree(jnp.maximum, s), axis=0, keepdims=True)
        p = [jnp.exp2(blk - m) for blk in s]
        for i, blk in enumerate(p):
            p_ref[slot, i * ATT_ROWS:(i + 1) * ATT_ROWS, :] = blk.astype(BF16)
        return jnp.sum(_tree(jnp.add, p), axis=0, keepdims=True)

    def values(qi, l):
        lo, hi = qi * t, (qi + 1) * t
        acc = _dot(vt_ref[:, 0:hi], p_ref[qi % ATT_SLOTS, 0:hi, :])
        o_ref[lo:hi, :] = (acc / l).T.astype(o_ref.dtype)

    for qi in range(min(ATT_AHEAD, n_q)):
        scores(qi)
    for qi in range(n_q):
        if qi + ATT_AHEAD < n_q:
            scores(qi + ATT_AHEAD)
        values(qi, softmax(qi))


def _attn_prefill(q, k, vt):
    b, s, _ = q.shape
    return pl.pallas_call(
        _attn_prefill_kernel,
        grid=(b, N_HEADS),
        in_specs=[
            pl.BlockSpec((None, s, HEAD_PAD), lambda bi, h: (bi, 0, h)),
            pl.BlockSpec((None, s, HEAD_PAD), lambda bi, h: (bi, 0, h)),
            pl.BlockSpec((None, V_HEAD, s), lambda bi, h: (bi, h, 0)),
        ],
        out_specs=pl.BlockSpec((None, s, V_HEAD), lambda bi, h: (bi, 0, h)),
        out_shape=jax.ShapeDtypeStruct((b, s, N_HEADS * V_HEAD), BF16),
        scratch_shapes=[pltpu.VMEM((ATT_SLOTS, s, ATT_TILE), F32), pltpu.VMEM((ATT_SLOTS, s, ATT_TILE), BF16)],
        compiler_params=_params("parallel", "parallel"),
        name="attn_prefill",
    )(q, k, vt)


def _attn_decode_kernel(q_ref, k_ref, v_ref, o_ref, *, q_start, t_valid):
    s_q, t = q_ref.shape[0], k_ref.shape[0]
    q_chunk = (q_start + lax.broadcasted_iota(jnp.int32, (s_q, t), 0)) // CHUNK
    k_pos = lax.broadcasted_iota(jnp.int32, (s_q, t), 1)
    visible = (k_pos // CHUNK <= q_chunk) & (k_pos < t_valid)
    for hd in range(N_HEADS):
        qk = slice(hd * HEAD_PAD, (hd + 1) * HEAD_PAD)
        vo = slice(hd * V_HEAD, (hd + 1) * V_HEAD)
        s = jnp.where(visible, _scores(q_ref[:, qk], k_ref[:, qk]), -jnp.inf)
        p = jnp.exp2(s - jnp.max(s, axis=1, keepdims=True))
        l = jnp.sum(p, axis=1, keepdims=True)
        o_ref[:, vo] = (_dot(p.astype(BF16), v_ref[:, vo]) / l).astype(o_ref.dtype)


def _attn_decode(q, k, v, q_start, t_valid):
    b, s, _ = q.shape
    t = k.shape[1]
    seq = lambda bi: (bi, 0, 0)
    return pl.pallas_call(
        functools.partial(_attn_decode_kernel, q_start=q_start, t_valid=t_valid),
        grid=(b,),
        in_specs=[
            pl.BlockSpec((None, s, N_HEADS * HEAD_PAD), seq),
            pl.BlockSpec((None, t, N_HEADS * HEAD_PAD), seq),
            pl.BlockSpec((None, t, N_HEADS * V_HEAD), seq),
        ],
        out_specs=pl.BlockSpec((None, s, N_HEADS * V_HEAD), seq),
        out_shape=jax.ShapeDtypeStruct((b, s, N_HEADS * V_HEAD), BF16),
        compiler_params=_params("parallel"),
        name="attn_decode",
    )(q, k, v)


def _router_kernel(x_ref, wh_ref, wl_ref, comb_ref, code_ref, cnt_ref):
    x = x_ref[...]
    xh = x.astype(BF16)
    xl = (x - xh.astype(F32)).astype(BF16)
    wh = wh_ref[...]
    logits = _dot(xh, wh) + (_dot(xl, wh) + _dot(xh, wl_ref[...]))
    tm = logits.shape[0]
    lane = lax.broadcasted_iota(jnp.int32, logits.shape, 1)
    logits = jnp.where(lane < N_EXPERTS, logits, -jnp.inf)
    m1 = jnp.max(logits, axis=1, keepdims=True)
    i1 = jnp.min(jnp.where(logits == m1, lane, LANES), axis=1, keepdims=True)
    rest = jnp.where(lane == i1, -jnp.inf, logits)
    m2 = jnp.max(rest, axis=1, keepdims=True)
    i2 = jnp.min(jnp.where(rest == m2, lane, LANES), axis=1, keepdims=True)
    e2 = jnp.exp(m2 - m1)
    w1 = 1.0 / (1.0 + e2)
    comb_ref[...] = jnp.where(lane == i1, w1, jnp.where(lane == i2, e2 * w1, 0.0))
    chosen = (lane == i1) | (lane == i2)
    onehot = jnp.where(chosen, 1.0, 0.0)
    earlier = lax.broadcasted_iota(jnp.int32, (tm, tm), 0) > lax.broadcasted_iota(jnp.int32, (tm, tm), 1)
    rank = _dot(jnp.where(earlier, 1.0, 0.0).astype(BF16), onehot.astype(BF16))
    code_ref[...] = jnp.where(chosen, rank, -1.0)
    cnt_ref[...] = jnp.broadcast_to(jnp.sum(onehot, axis=0, keepdims=True), cnt_ref.shape)


def _router(x, w_hi, w_lo, tm):
    n = x.shape[0]
    row = lambda i: (i, 0)
    return pl.pallas_call(
        _router_kernel,
        grid=(n // tm,),
        in_specs=[
            pl.BlockSpec((tm, D_MODEL), row),
            pl.BlockSpec(w_hi.shape, lambda i: (0, 0)),
            pl.BlockSpec(w_lo.shape, lambda i: (0, 0)),
        ],
        out_specs=[
            pl.BlockSpec((tm, LANES), row),
            pl.BlockSpec((tm, LANES), row),
            pl.BlockSpec((8, LANES), row),
        ],
        out_shape=[
            jax.ShapeDtypeStruct((n, LANES), F32),
            jax.ShapeDtypeStruct((n, LANES), F32),
            jax.ShapeDtypeStruct((n // tm * 8, LANES), F32),
        ],
        compiler_params=_params("parallel"),
        name="router",
    )(x, w_hi, w_lo)


def _gmlp_kernel(x_ref, win_ref, g_ref, b_ref, mix_ref, bias_ref, wout_ref, ln_g_ref, ln_b_ref,
                 o_ref, v_ref, *, tm, v_rows, v_every):
    x = x_ref[...]
    h = jax.nn.gelu(_dot(x.astype(BF16), win_ref[...]))
    u = h[:, :D_MODEL]
    v = _layer_norm(h[:, D_MODEL:], g_ref[...], b_ref[...])
    vb = v.astype(BF16)
    bias = bias_ref[...]
    gated = []
    for r in range(tm // GM_MIX):
        rows = slice(r * GM_MIX, (r + 1) * GM_MIX)
        cols = []
        for grp in range(GM_GROUPS):
            lanes = slice(grp * GM_GROUP_DIM, (grp + 1) * GM_GROUP_DIM)
            cols.append(_dot(mix_ref[grp], vb[rows, lanes]))
        mixed = jnp.concatenate(cols, axis=1) + bias
        gated.append((u[rows] * mixed).astype(BF16))
    y = _dot(jnp.concatenate(gated, axis=0), wout_ref[...])
    o_ref[...] = _layer_norm(ALPHA * x + y, ln_g_ref[...], ln_b_ref[...])

    @pl.when(pl.program_id(0) % v_every == v_every - 1)
    def _():
        v_ref[...] = v[tm - v_rows:, :]


def _gmlp(x, w, mix, bias, ln_g, ln_b, tm, v_rows, v_every):
    n = x.shape[0]
    row = lambda i: (i, 0)
    const = lambda i: (0, 0)
    n_v = n // (tm * v_every) * v_rows
    return pl.pallas_call(
        functools.partial(_gmlp_kernel, tm=tm, v_rows=v_rows, v_every=v_every),
        grid=(n // tm,),
        in_specs=[
            pl.BlockSpec((tm, D_MODEL), row),
            pl.BlockSpec(w['w_in'].shape, const),
            pl.BlockSpec((1, D_MODEL), const),
            pl.BlockSpec((1, D_MODEL), const),
            pl.BlockSpec(mix.shape, lambda i: (0, 0, 0)),
            pl.BlockSpec(bias.shape, const),
            pl.BlockSpec(w['w_out'].shape, const),
            pl.BlockSpec((1, D_MODEL), const),
            pl.BlockSpec((1, D_MODEL), const),
        ],
        out_specs=[
            pl.BlockSpec((tm, D_MODEL), row),
            pl.BlockSpec((v_rows, D_MODEL), lambda i: (i // v_every, 0)),
        ],
        out_shape=[
            jax.ShapeDtypeStruct((n, D_MODEL), F32),
            jax.ShapeDtypeStruct((n_v, D_MODEL), F32),
        ],
        compiler_params=_params("arbitrary"),
        name="gmlp",
    )(x, w['w_in'], w['g_norm'], w['b_norm'], mix, bias, w['w_out'], ln_g, ln_b)


def _ln_ple(z, g_ref, b_ref, p_ref, wpg_ref, wpp_ref):
    x2 = _layer_norm(z, g_ref[...], b_ref[...])
    gate = jax.nn.sigmoid(_dot(x2.astype(BF16), wpg_ref[...]))
    return x2 + gate * _dot(p_ref[...].astype(BF16), wpp_ref[...])


def _swiglu(xb, wg_ref, wu_ref, wd_ref):
    d_ff = wg_ref.shape[1]
    y = None
    for lo in range(0, d_ff, FF_CHUNK):
        hi = min(lo + FF_CHUNK, d_ff)
        h = jax.nn.silu(_dot(xb, wg_ref[:, lo:hi])) * _dot(xb, wu_ref[:, lo:hi])
        d = _dot(h.astype(BF16), wd_ref[lo:hi, :])
        y = d if y is None else y + d
    return y


def _attn_ffn_kernel(a_ref, x_ref, wo_ref, g1_ref, b1_ref, wg_ref, wu_ref, wd_ref, g_ref, b_ref, p_ref,
                     wpg_ref, wpp_ref, o_ref):
    x1 = _layer_norm(ALPHA * x_ref[...] + _dot(a_ref[...], wo_ref[...]), g1_ref[...], b1_ref[...])
    y = _swiglu(x1.astype(BF16), wg_ref, wu_ref, wd_ref)
    o_ref[...] = _ln_ple(ALPHA * x1 + y, g_ref, b_ref, p_ref, wpg_ref, wpp_ref)


def _resident(shape):
    return pl.BlockSpec(shape, lambda i: (0,) * len(shape), pipeline_mode=pl.Buffered(1))


def _attn_ffn(a, x, w_out, ln1_g, ln1_b, wg, wu, wd, ln_g, ln_b, p, layer, w_ple_gate, w_ple_proj, tm):
    n = x.shape[0]
    row = lambda i: (i, 0)
    const = lambda i: (0, 0)
    return pl.pallas_call(
        _attn_ffn_kernel,
        grid=(n // tm,),
        in_specs=[
            pl.BlockSpec((tm, a.shape[1]), row),
            pl.BlockSpec((tm, D_MODEL), row),
            _resident(w_out.shape),
            pl.BlockSpec((1, D_MODEL), const),
            pl.BlockSpec((1, D_MODEL), const),
            _resident(wg.shape),
            _resident(wu.shape),
            _resident(wd.shape),
            pl.BlockSpec((1, D_MODEL), const),
            pl.BlockSpec((1, D_MODEL), const),
            pl.BlockSpec((None, tm, p.shape[2]), lambda i: (layer, i, 0)),
            _resident(w_ple_gate.shape),
            _resident(w_ple_proj.shape),
        ],
        out_specs=pl.BlockSpec((tm, D_MODEL), row),
        out_shape=jax.ShapeDtypeStruct((n, D_MODEL), F32),
        compiler_params=_params("parallel"),
        name="attn_ffn",
    )(a, x, w_out, ln1_g, ln1_b, wg, wu, wd, ln_g, ln_b, p, w_ple_gate, w_ple_proj)


def _moe_plan(cnt, n_tiles, tm_ffn, n_ffn_tiles):
    cnt = cnt.reshape(n_tiles, 8, LANES)[:, 0, :N_EXPERTS].astype(jnp.int32)
    nch = (cnt + MOE_CHUNK - 1) // MOE_CHUNK
    base = jnp.cumsum(nch, axis=1) - nch
    seg = (cnt + BF16_ROWS - 1) // BF16_ROWS * BF16_ROWS
    content = jnp.sum(seg, axis=0)
    group_pad = (content + MOE_CHUNK + tm_ffn - 1) // tm_ffn * tm_ffn
    group_start = jnp.cumsum(group_pad) - group_pad
    seg_start = group_start[None, :] + jnp.cumsum(seg, axis=0) - seg
    k = jnp.arange(MOE_SLOTS)[None, :, None]
    lo, hi = base[:, None, :], (base + nch)[:, None, :]
    slot_dst = jnp.sum(jnp.where((k >= lo) & (k < hi), seg_start[:, None, :] + (k - lo) * MOE_CHUNK, 0), axis=2)
    tile_end = jnp.cumsum(group_pad // tm_ffn)
    n_used = tile_end[-1]
    tile_id = jnp.minimum(jnp.arange(n_ffn_tiles), n_used - 1)
    tile_expert = jnp.sum(tile_id[:, None] >= tile_end[None, :], axis=1)
    base_rows = jnp.zeros((n_tiles, 8, LANES), F32).at[:, :, :N_EXPERTS].set(
        (base * MOE_CHUNK).astype(F32)[:, None, :]).reshape(n_tiles * 8, LANES)
    flat = lambda a: a.reshape(-1).astype(jnp.int32)
    used_rows = n_used * tm_ffn
    zero_row = jnp.concatenate([group_start + content, used_rows[None]])
    zero_cnt = jnp.concatenate([group_pad - content, (n_ffn_tiles * tm_ffn - used_rows)[None]]) // BF16_ROWS
    return dict(slot_dst=flat(slot_dst), n_chunks=flat(jnp.sum(nch, axis=1)), base_rows=base_rows,
                zero_row=flat(zero_row), zero_cnt=flat(zero_cnt),
                tile_expert=flat(tile_expert), n_used=flat(n_used))


def _staged_rows(code, base_rows):
    pos = jnp.where(code >= 0.0, base_rows + code, -1.0)
    first = jnp.max(pos, axis=1, keepdims=True)
    second = jnp.max(jnp.where(pos == first, -1.0, pos), axis=1, keepdims=True)
    return pos, first, second


def _chunk_copies(hbm_ref, vmem_ref, sem, dst_ref, nck_ref, tile, slot, to_hbm, action):
    def body(k, carry):
        stage = vmem_ref.at[slot, pl.ds(pl.multiple_of(k * MOE_CHUNK, MOE_CHUNK), MOE_CHUNK)]
        group = hbm_ref.at[pl.ds(pl.multiple_of(dst_ref[tile * MOE_SLOTS + k], BF16_ROWS), MOE_CHUNK)]
        src, dst = (stage, group) if to_hbm else (group, stage)
        copy = pltpu.make_async_copy(src, dst, sem.at[slot])
        copy.start() if action == "start" else copy.wait()
        return carry

    lax.fori_loop(0, nck_ref[tile], body, 0)


def _zero_fill(xs_ref, zero_ref, sem, zrow_ref, zcnt_ref, action):
    for e in range(N_EXPERTS + 1):
        def body(k, carry):
            rows = xs_ref.at[pl.ds(pl.multiple_of(zrow_ref[e] + k * BF16_ROWS, BF16_ROWS), BF16_ROWS)]
            copy = pltpu.make_async_copy(zero_ref, rows, sem.at[2])
            copy.start() if action == "start" else copy.wait()
            return carry

        lax.fori_loop(0, zcnt_ref[e], body, 0)


def _dispatch_kernel(dst_ref, nck_ref, zrow_ref, zcnt_ref, x_ref, code_ref, brow_ref, xs_ref,
                     stage_ref, zero_ref, sem):
    t = pl.program_id(0)
    slot = t % 2
    _, first, second = _staged_rows(code_ref[...], brow_ref[0:1, :])
    lane = lax.broadcasted_iota(jnp.int32, (TOKEN_TILE, LANES), 1)
    rows_t = jnp.where(lane == 0, first, jnp.where(lane == 1, second, -1.0)).T
    r1, r2 = rows_t[0:1, :], rows_t[1:2, :]
    xb = x_ref[...].astype(BF16)
    n_chunks = nck_ref[t]
    for g in range(MOE_STAGE // MOE_GROUP):
        @pl.when(g * (MOE_GROUP // MOE_CHUNK) < n_chunks)
        def _():
            row = (lax.broadcasted_iota(jnp.int32, (MOE_GROUP, TOKEN_TILE), 0) + g * MOE_GROUP).astype(F32)
            sel = jnp.where((row == r1) | (row == r2), 1.0, 0.0).astype(BF16)
            stage_ref[slot, g * MOE_GROUP:(g + 1) * MOE_GROUP, :] = _dot(sel, xb).astype(BF16)

    copies = functools.partial(_chunk_copies, xs_ref, stage_ref, sem, dst_ref, nck_ref, to_hbm=True)

    @pl.when(t > 0)
    def _():
        copies(tile=t - 1, slot=1 - slot, action="wait")

    copies(tile=t, slot=slot, action="start")

    @pl.when(t == pl.num_programs(0) - 1)
    def _():
        copies(tile=t, slot=slot, action="wait")
        zero_ref[...] = jnp.zeros(zero_ref.shape, BF16)
        _zero_fill(xs_ref, zero_ref, sem, zrow_ref, zcnt_ref, "start")
        _zero_fill(xs_ref, zero_ref, sem, zrow_ref, zcnt_ref, "wait")


def _dispatch(x, code, plan, n_rows):
    n = x.shape[0]
    row = lambda i, *_: (i, 0)
    grid_spec = pltpu.PrefetchScalarGridSpec(
        num_scalar_prefetch=4,
        grid=(n // TOKEN_TILE,),
        in_specs=[
            pl.BlockSpec((TOKEN_TILE, D_MODEL), row),
            pl.BlockSpec((TOKEN_TILE, LANES), row),
            pl.BlockSpec((8, LANES), row),
        ],
        out_specs=pl.BlockSpec(memory_space=pl.ANY),
        scratch_shapes=[pltpu.VMEM((2, MOE_STAGE, D_MODEL), BF16), pltpu.VMEM((BF16_ROWS, D_MODEL), BF16),
                        pltpu.SemaphoreType.DMA((3,))],
    )
    return pl.pallas_call(
        _dispatch_kernel,
        grid_spec=grid_spec,
        out_shape=jax.ShapeDtypeStruct((n_rows, D_MODEL), BF16),
        compiler_params=_params("arbitrary"),
        name="moe_dispatch",
    )(plan['slot_dst'], plan['n_chunks'], plan['zero_row'], plan['zero_cnt'], x, code, plan['base_rows'])


def _moe_ffn_kernel(te_ref, nu_ref, x_ref, wg_ref, wu_ref, wd_ref, o_ref, acc_ref):
    del te_ref
    f = pl.program_id(1)
    used = pl.program_id(0) < nu_ref[0]

    @pl.when(jnp.logical_not(used) & (f == 0))
    def _():
        o_ref[...] = jnp.zeros(o_ref.shape, o_ref.dtype)

    @pl.when(used)
    def _():
        @pl.when(f == 0)
        def _():
            acc_ref[...] = jnp.zeros(acc_ref.shape, F32)

        acc_ref[...] += _swiglu(x_ref[...], wg_ref, wu_ref, wd_ref)

        @pl.when(f == pl.num_programs(1) - 1)
        def _():
            o_ref[...] = acc_ref[...].astype(o_ref.dtype)


def _moe_ffn(xs, plan, wg, wu, wd, layer, tm, tf):
    n_rows = xs.shape[0]
    n_f = wg.shape[-1] // tf
    tile = lambda i, f, te, nu: (jnp.minimum(i, nu[0] - 1), 0)
    f_idx = lambda i, f, nu: jnp.where(i < nu[0], f, n_f - 1)
    grid_spec = pltpu.PrefetchScalarGridSpec(
        num_scalar_prefetch=2,
        grid=(n_rows // tm, n_f),
        in_specs=[
            pl.BlockSpec((tm, D_MODEL), tile),
            pl.BlockSpec((None, None, D_MODEL, tf), lambda i, f, te, nu: (layer, te[i], 0, f_idx(i, f, nu))),
            pl.BlockSpec((None, None, D_MODEL, tf), lambda i, f, te, nu: (layer, te[i], 0, f_idx(i, f, nu))),
            pl.BlockSpec((None, None, tf, D_MODEL), lambda i, f, te, nu: (layer, te[i], f_idx(i, f, nu), 0)),
        ],
        out_specs=pl.BlockSpec((tm, D_MODEL), lambda i, f, te, nu: (i, 0)),
        scratch_shapes=[pltpu.VMEM((tm, D_MODEL), F32)],
    )
    return pl.pallas_call(
        _moe_ffn_kernel,
        grid_spec=grid_spec,
        out_shape=jax.ShapeDtypeStruct((n_rows, D_MODEL), BF16),
        compiler_params=_params("arbitrary", "arbitrary"),
        name="moe_ffn",
    )(plan['tile_expert'], plan['n_used'], xs, wg, wu, wd)


def _combine_kernel(dst_ref, nck_ref, x_ref, comb_ref, code_ref, brow_ref, ys_ref,
                    g_ref, b_ref, p_ref, wpg_ref, wpp_ref, o_ref, buf_ref, y_ref, sem):
    t = pl.program_id(0)
    slot = t % 2
    copies = functools.partial(_chunk_copies, ys_ref, buf_ref, sem, dst_ref, nck_ref, to_hbm=False)

    @pl.when(t == 0)
    def _():
        buf_ref[...] = jnp.zeros(buf_ref.shape, BF16)
        copies(tile=t, slot=slot, action="start")

    @pl.when(t + 1 < pl.num_programs(0))
    def _():
        copies(tile=t + 1, slot=1 - slot, action="start")

    copies(tile=t, slot=slot, action="wait")

    comb = comb_ref[...]
    pos, first, second = _staged_rows(code_ref[...], brow_ref[0:1, :])
    w_first = jnp.max(jnp.where(pos == first, comb, 0.0), axis=1, keepdims=True)
    w_second = jnp.max(jnp.where(pos == second, comb, 0.0), axis=1, keepdims=True)
    y_ref[...] = jnp.zeros(y_ref.shape, F32)
    n_chunks = nck_ref[t]
    for g in range(MOE_STAGE // MOE_GROUP):
        @pl.when(g * (MOE_GROUP // MOE_CHUNK) < n_chunks)
        def _():
            row = (lax.broadcasted_iota(jnp.int32, (TOKEN_TILE, MOE_GROUP), 1) + g * MOE_GROUP).astype(F32)
            rows = buf_ref[slot, g * MOE_GROUP:(g + 1) * MOE_GROUP, :]
            y_ref[...] += (w_first * _dot(jnp.where(row == first, 1.0, 0.0).astype(BF16), rows)
                           + w_second * _dot(jnp.where(row == second, 1.0, 0.0).astype(BF16), rows))

    o_ref[...] = _ln_ple(ALPHA * x_ref[...] + y_ref[...], g_ref, b_ref, p_ref, wpg_ref, wpp_ref)


def _combine(x, comb, code, plan, ys, ln_g, ln_b, p, layer, w_ple_gate, w_ple_proj):
    n = x.shape[0]
    row = lambda i, *_: (i, 0)
    const = lambda i, *_: (0, 0)
    grid_spec = pltpu.PrefetchScalarGridSpec(
        num_scalar_prefetch=2,
        grid=(n // TOKEN_TILE,),
        in_specs=[
            pl.BlockSpec((TOKEN_TILE, D_MODEL), row),
            pl.BlockSpec((TOKEN_TILE, LANES), row),
            pl.BlockSpec((TOKEN_TILE, LANES), row),
            pl.BlockSpec((8, LANES), row),
            pl.BlockSpec(memory_space=pl.ANY),
            pl.BlockSpec((1, D_MODEL), const),
            pl.BlockSpec((1, D_MODEL), const),
            pl.BlockSpec((None, TOKEN_TILE, p.shape[2]), lambda i, *_: (layer, i, 0)),
            pl.BlockSpec(w_ple_gate.shape, const),
            pl.BlockSpec(w_ple_proj.shape, const),
        ],
        out_specs=pl.BlockSpec((TOKEN_TILE, D_MODEL), row),
        scratch_shapes=[pltpu.VMEM((2, MOE_STAGE, D_MODEL), BF16), pltpu.VMEM((TOKEN_TILE, D_MODEL), F32),
                        pltpu.SemaphoreType.DMA((2,))],
    )
    return pl.pallas_call(
        _combine_kernel,
        grid_spec=grid_spec,
        out_shape=jax.ShapeDtypeStruct((n, D_MODEL), F32),
        compiler_params=_params("arbitrary"),
        name="moe_combine",
    )(plan['slot_dst'], plan['n_chunks'], x, comb, code, plan['base_rows'], ys,
      ln_g, ln_b, p, w_ple_gate, w_ple_proj)


def _moe(x, comb, code, cnt, wg, wu, wd, moe_layer, ln_g, ln_b, p, layer, w_ple_gate, w_ple_proj):
    n = x.shape[0]
    n_tiles = n // TOKEN_TILE
    tm_ffn = 1024 if n >= 8192 else 256
    max_rows = TOP_K * n + n_tiles * N_EXPERTS * (BF16_ROWS - 1) + N_EXPERTS * (MOE_CHUNK + tm_ffn - 1)
    n_ffn_tiles = -(-max_rows // tm_ffn)
    plan = _moe_plan(cnt, n_tiles, tm_ffn, n_ffn_tiles)
    xs = _dispatch(x, code, plan, n_ffn_tiles * tm_ffn)
    ys = _moe_ffn(xs, plan, wg, wu, wd, moe_layer, tm_ffn, MOE_FF_BLOCK)
    return _combine(x, comb, code, plan, ys, ln_g, ln_b, p, layer, w_ple_gate, w_ple_proj)


def _rope_tables(pos):
    inv = ROPE_THETA ** (-jnp.arange(ROPE_HALF, dtype=F32) / ROPE_HALF)
    ang = pos.astype(F32)[:, None] * inv[None, :]
    cos, sin, zero = jnp.cos(ang), jnp.sin(ang), jnp.zeros_like(ang)
    return (jnp.concatenate([cos, cos, zero, zero], axis=1),
            jnp.concatenate([-sin, sin, zero, zero], axis=1))


def _prep_mla(w_in, g_q, w_q_up, g_kv, w_kv_up, w_out):
    k_r = w_in[:, Q_LORA + KV_LORA:]
    w_in_p = jnp.concatenate([w_in[:, :Q_LORA + KV_LORA], k_r, k_r], axis=1)
    wq = w_q_up.reshape(Q_LORA, N_HEADS, QK_NOPE + QK_ROPE)
    wq_p = jnp.concatenate([wq, wq[:, :, QK_NOPE:]], axis=2).reshape(Q_LORA, N_HEADS * HEAD_PAD)
    wkv = w_kv_up.reshape(KV_LORA, N_HEADS, QK_NOPE + V_HEAD)
    w_k = wkv[:, :, :QK_NOPE].reshape(KV_LORA, -1).astype(BF16)
    w_v = wkv[:, :, QK_NOPE:].reshape(KV_LORA, -1).astype(BF16)
    return dict(w_in=w_in_p.astype(BF16), g_q=g_q[None, :], w_q_up=wq_p.astype(BF16), g_kv=g_kv[None, :],
                w_k=w_k, w_v=w_v, w_v_t=w_v.T, w_out=w_out.astype(BF16))


def _prep_gmlp_mix(w_s, b_s, chunk):
    wm = jnp.tril(w_s[:, :chunk, :chunk])
    reps = GM_MIX // chunk
    mix = jnp.einsum('ab,gts->gatbs', jnp.eye(reps, dtype=F32), wm).reshape(GM_GROUPS, GM_MIX, GM_MIX)
    bias = jnp.repeat(jnp.tile(b_s[:, :chunk].T, (reps, 1)), GM_GROUP_DIM, axis=1)
    return mix.astype(BF16), bias


def _row(v):
    return v[None, :]


def _trunk(x, p, past_latent, past_krope, W):
    b, s, _ = x.shape
    n = b * s
    tm = TOKEN_TILE
    past = 0 if past_latent is None else past_latent.shape[2]
    cos_t, sin_t = _rope_tables(past + jnp.arange(s))
    if s < tm:
        cos_t, sin_t = jnp.tile(cos_t, (tm // s, 1)), jnp.tile(sin_t, (tm // s, 1))
    x = x.reshape(n, D_MODEL)
    p = p.reshape(DEPTH, n, -1)
    chunk = min(s, GM_CHUNK)
    new_lat, new_kr, new_v = [], [], []
    for i in range(DEPTH):
        j = i // 2
        ln1 = (_row(W['ln1_g'][i]), _row(W['ln1_b'][i]))
        tail = (_row(W['ln2_g'][i]), _row(W['ln2_b'][i]), p, i, W['w_ple_gate'][i], W['w_ple_proj'][i])
        if i % 2 == 0:
            w = W['mla'][j]
            if past_latent is None:
                q, lat, kr, k, vt = _mla_proj(x, w, cos_t, sin_t, tm, True)
                att = _attn_prefill(q.reshape(b, s, -1), k.reshape(b, s, -1), vt)
                lat = lat.reshape(b, s, KV_LORA)
            else:
                q, lat, kr, krp = _mla_proj(x, w, cos_t, sin_t, tm, False)
                q = q.reshape(b, s, -1)
                lat = lat.reshape(b, s, KV_LORA)
                krp = krp.reshape(b, s, LANES)
                t_valid = past + s
                t_pad = -(-t_valid // LANES) * LANES
                lat_all = jnp.concatenate(
                    [past_latent[j], lat, jnp.zeros((b, t_pad - t_valid, KV_LORA), F32)], axis=1)
                krp_all = jnp.concatenate(
                    [jnp.pad(past_krope[j], ((0, 0), (0, 0), (0, LANES - QK_ROPE))), krp,
                     jnp.zeros((b, t_pad - t_valid, LANES), F32)], axis=1)
                k, v = _kv_up(lat_all, krp_all, w, t_pad)
                att = _attn_decode(q, k, v, past, t_valid)
            new_lat.append(lat)
            new_kr.append(kr.reshape(b, s, QK_ROPE))
            x = _attn_ffn(att.reshape(n, -1), x, w['w_out'], *ln1,
                          W['w_ffn_gate'][j], W['w_ffn_up'][j], W['w_ffn_down'][j], *tail, tm)
        else:
            w = W['gm'][j]
            mix, bias = _prep_gmlp_mix(w['w_spatial'], w['b_spatial'], chunk)
            v_rows, v_every = (GM_CHUNK, s // tm) if s >= GM_CHUNK else (tm, 1)
            x, v_new = _gmlp(x, w, mix, bias, *ln1, tm, v_rows, v_every)
            new_v.append(v_new.reshape(b, chunk, D_MODEL))
            comb, code, cnt = _router(x, W['w_router_hi'][j], W['w_router_lo'][j], tm)
            x = _moe(x, comb, code, cnt, W['w_moe_gate'], W['w_moe_up'], W['w_moe_down'], j, *tail)
    return x.reshape(b, s, D_MODEL), jnp.stack(new_lat), jnp.stack(new_kr), jnp.stack(new_v)


def kernel(x_prompt, x_sample, cache_mla_latent, cache_mla_krope, p_prompt, p_sample, w_mla_in, g_q_norm, w_q_up, g_kv_norm, w_kv_up, w_mla_out, w_gm_in, g_gm_norm, b_gm_norm, w_gm_spatial, b_gm_spatial, w_gm_out, w_ffn_gate, w_ffn_up, w_ffn_down, w_router, w_moe_gate, w_moe_up, w_moe_down, ln1_g, ln1_b, ln2_g, ln2_b, w_ple_gate, w_ple_proj):
    n_mla, n_gm = w_mla_in.shape[0], w_gm_in.shape[0]
    w_router = jnp.pad(w_router, ((0, 0), (0, 0), (0, LANES - N_EXPERTS)))
    w_router_hi = w_router.astype(BF16)
    W = dict(
        mla=[_prep_mla(w_mla_in[j], g_q_norm[j], w_q_up[j], g_kv_norm[j], w_kv_up[j], w_mla_out[j])
             for j in range(n_mla)],
        gm=[dict(w_in=w_gm_in[j].astype(BF16), g_norm=_row(g_gm_norm[j]), b_norm=_row(b_gm_norm[j]),
                 w_spatial=w_gm_spatial[j], b_spatial=b_gm_spatial[j], w_out=w_gm_out[j].astype(BF16))
            for j in range(n_gm)],
        w_ffn_gate=w_ffn_gate.astype(BF16), w_ffn_up=w_ffn_up.astype(BF16), w_ffn_down=w_ffn_down.astype(BF16),
        w_router_hi=w_router_hi, w_router_lo=(w_router - w_router_hi.astype(F32)).astype(BF16),
        w_moe_gate=w_moe_gate.astype(BF16), w_moe_up=w_moe_up.astype(BF16), w_moe_down=w_moe_down.astype(BF16),
        ln1_g=ln1_g, ln1_b=ln1_b, ln2_g=ln2_g, ln2_b=ln2_b,
        w_ple_gate=w_ple_gate.astype(BF16), w_ple_proj=w_ple_proj.astype(BF16),
    )
    y_p, lat_p, kr_p, v_p = _trunk(x_prompt, p_prompt, None, None, W)
    y_s, lat_s, kr_s, v_s = _trunk(x_sample, p_sample, cache_mla_latent, cache_mla_krope, W)
    return (y_p, y_s, lat_p, kr_p, lat_s, kr_s, v_p, v_s)
```

```python
import functools

import jax
import jax.numpy as jnp
from jax import lax
from jax.experimental import pallas as pl
from jax.experimental.pallas import tpu as pltpu

F32 = jnp.float32
BF16 = jnp.bfloat16

D_MODEL = 1024
DEPTH = 4
CHUNK = 64
N_HEADS = 8
QK_NOPE = 128
QK_ROPE = 64
V_HEAD = 128
Q_LORA = 384
KV_LORA = 256
ROPE_THETA = 10000.0
ATTN_SCALE = (QK_NOPE + QK_ROPE) ** -0.5
GM_CHUNK = 128
GM_GROUPS = 8
GM_GROUP_DIM = D_MODEL // GM_GROUPS
N_EXPERTS = 8
TOP_K = 2
ALPHA = (2 * DEPTH) ** 0.25

LANES = 128
BF16_ROWS = 16
HEAD_PAD = 2 * LANES
ROPE_HALF = QK_ROPE // 2
LOG2E = 1.4426950408889634
Q_SCALE = ATTN_SCALE * LOG2E
ATT_TILE = 256
ATT_ROWS = 32
ATT_AHEAD = 3
ATT_SLOTS = ATT_AHEAD + 1
GM_MIX = 256
FF_CHUNK = 1024
MOE_FF_BLOCK = 1792
TOKEN_TILE = 512
MOE_CHUNK = 64
MOE_SLOTS = TOP_K * TOKEN_TILE // MOE_CHUNK + N_EXPERTS
MOE_STAGE = MOE_SLOTS * MOE_CHUNK
MOE_GROUP = 256
MOE_ALWAYS = TOP_K * TOKEN_TILE
VMEM_LIMIT = 56 * 1024 * 1024


def _params(*sem):
    return pltpu.CompilerParams(dimension_semantics=sem, vmem_limit_bytes=VMEM_LIMIT)


def _dot(a, b):
    return jnp.dot(a, b, preferred_element_type=F32)


def _layer_norm(z, g, b):
    mu = jnp.mean(z, -1, keepdims=True)
    zc = z - mu
    var = jnp.mean(zc * zc, -1, keepdims=True)
    return zc * lax.rsqrt(var + 1e-5) * g + b


def _rms_norm(z, g):
    return z * lax.rsqrt(jnp.mean(z * z, -1, keepdims=True) + 1e-6) * g


def _rope(r, cos_t, sin_t):
    return r * cos_t + pltpu.roll(r, ROPE_HALF, 1) * sin_t


def _write_kv(lat, krp, wk_ref, wv_ref, k_ref, v_ref, transpose_v):
    kn = _dot(lat, wk_ref[...])
    for hd in range(N_HEADS):
        lo = hd * HEAD_PAD
        k_ref[:, lo:lo + LANES] = kn[:, hd * LANES:(hd + 1) * LANES].astype(BF16)
        k_ref[:, lo + LANES:lo + HEAD_PAD] = krp
    if transpose_v:
        v_ref[...] = lax.dot_general(wv_ref[...], lat, (((1,), (1,)), ((), ())),
                                     preferred_element_type=F32).astype(BF16)
    else:
        v_ref[...] = _dot(lat, wv_ref[...]).astype(BF16)


def _mla_proj_kernel(*refs, with_kv):
    x_ref, win_ref, gq_ref, wq_ref, gkv_ref, cos_ref, sin_ref = refs[:7]
    if with_kv:
        wk_ref, wvt_ref, q_ref, lat_ref, kr_ref, k_ref, vt_ref = refs[7:]
    else:
        q_ref, lat_ref, kr_ref, krp_ref = refs[7:]
    xb = x_ref[...].astype(BF16)
    h = _dot(xb, win_ref[...])
    cos_t = cos_ref[...]
    sin_t = sin_ref[...]
    cq = _rms_norm(h[:, :Q_LORA], gq_ref[...])
    q = _dot(cq.astype(BF16), wq_ref[...])
    for hd in range(N_HEADS):
        lo = hd * HEAD_PAD
        q_ref[:, lo:lo + LANES] = (q[:, lo:lo + LANES] * Q_SCALE).astype(BF16)
        rot = _rope(q[:, lo + LANES:lo + HEAD_PAD], cos_t, sin_t) * Q_SCALE
        q_ref[:, lo + LANES:lo + HEAD_PAD] = rot.astype(BF16)
    lat = _rms_norm(h[:, Q_LORA:Q_LORA + KV_LORA], gkv_ref[...])
    lat_ref[...] = lat
    krot = _rope(h[:, Q_LORA + KV_LORA:], cos_t, sin_t)
    kr_ref[...] = krot[:, :QK_ROPE]
    if with_kv:
        _write_kv(lat.astype(BF16), krot.astype(BF16), wk_ref, wvt_ref, k_ref, vt_ref, True)
    else:
        krp_ref[...] = krot


def _mla_proj(x, w, cos_t, sin_t, tm, with_kv):
    n = x.shape[0]
    per_seq = cos_t.shape[0] // tm
    row = lambda i: (i, 0)
    const = lambda i: (0, 0)
    in_specs = [
        pl.BlockSpec((tm, D_MODEL), row),
        pl.BlockSpec(w['w_in'].shape, const),
        pl.BlockSpec((1, Q_LORA), const),
        pl.BlockSpec(w['w_q_up'].shape, const),
        pl.BlockSpec((1, KV_LORA), const),
        pl.BlockSpec((tm, LANES), lambda i: (i % per_seq, 0)),
        pl.BlockSpec((tm, LANES), lambda i: (i % per_seq, 0)),
    ]
    args = [x, w['w_in'], w['g_q'], w['w_q_up'], w['g_kv'], cos_t, sin_t]
    out_specs = [
        pl.BlockSpec((tm, N_HEADS * HEAD_PAD), row),
        pl.BlockSpec((tm, KV_LORA), row),
        pl.BlockSpec((tm, QK_ROPE), row),
    ]
    out_shape = [
        jax.ShapeDtypeStruct((n, N_HEADS * HEAD_PAD), BF16),
        jax.ShapeDtypeStruct((n, KV_LORA), F32),
        jax.ShapeDtypeStruct((n, QK_ROPE), F32),
    ]
    if with_kv:
        in_specs += [pl.BlockSpec(w['w_k'].shape, const), pl.BlockSpec(w['w_v_t'].shape, const)]
        args += [w['w_k'], w['w_v_t']]
        out_specs += [pl.BlockSpec((tm, N_HEADS * HEAD_PAD), row),
                      pl.BlockSpec((None, N_HEADS * V_HEAD, tm), lambda i: (i // per_seq, 0, i % per_seq))]
        out_shape += [jax.ShapeDtypeStruct((n, N_HEADS * HEAD_PAD), BF16),
                      jax.ShapeDtypeStruct((n // (per_seq * tm), N_HEADS * V_HEAD, per_seq * tm), BF16)]
    else:
        out_specs.append(pl.BlockSpec((tm, LANES), row))
        out_shape.append(jax.ShapeDtypeStruct((n, LANES), F32))
    return pl.pallas_call(
        functools.partial(_mla_proj_kernel, with_kv=with_kv),
        grid=(n // tm,),
        in_specs=in_specs,
        out_specs=out_specs,
        out_shape=out_shape,
        compiler_params=_params("parallel"),
        name="mla_proj",
    )(*args)


def _kv_up_kernel(lat_ref, krp_ref, wk_ref, wv_ref, k_ref, v_ref):
    _write_kv(lat_ref[...].astype(BF16), krp_ref[...].astype(BF16), wk_ref, wv_ref, k_ref, v_ref, False)


def _kv_up(lat, krp, w, tt):
    b, t, _ = lat.shape
    blk = lambda bi, i: (bi, i, 0)
    return pl.pallas_call(
        _kv_up_kernel,
        grid=(b, t // tt),
        in_specs=[
            pl.BlockSpec((None, tt, KV_LORA), blk),
            pl.BlockSpec((None, tt, LANES), blk),
            pl.BlockSpec(w['w_k'].shape, lambda bi, i: (0, 0)),
            pl.BlockSpec(w['w_v'].shape, lambda bi, i: (0, 0)),
        ],
        out_specs=[pl.BlockSpec((None, tt, N_HEADS * HEAD_PAD), blk), pl.BlockSpec((None, tt, N_HEADS * V_HEAD), blk)],
        out_shape=[jax.ShapeDtypeStruct((b, t, N_HEADS * HEAD_PAD), BF16),
                   jax.ShapeDtypeStruct((b, t, N_HEADS * V_HEAD), BF16)],
        compiler_params=_params("parallel", "parallel"),
        name="kv_up",
    )(lat, krp, w['w_k'], w['w_v'])


def _scores(a, b):
    return lax.dot_general(a, b, (((1,), (1,)), ((), ())), preferred_element_type=F32)


def _tree(op, parts):
    while len(parts) > 1:
        parts = [op(parts[i], parts[i + 1]) if i + 1 < len(parts) else parts[i] for i in range(0, len(parts), 2)]
    return parts[0]


def _attn_prefill_kernel(q_ref, k_ref, vt_ref, o_ref, s_ref, p_ref):
    t = ATT_TILE
    n_q = q_ref.shape[0] // t
    k_chunk = lax.broadcasted_iota(jnp.int32, (t, t), 0) // CHUNK
    q_chunk = lax.broadcasted_iota(jnp.int32, (t, t), 1) // CHUNK
    diag_visible = k_chunk <= q_chunk

    def scores(qi):
        lo, hi = qi * t, (qi + 1) * t
        s_ref[qi % ATT_SLOTS, 0:hi, :] = _scores(k_ref[0:hi, :], q_ref[lo:hi, :])

    def softmax(qi):
        lo, slot = qi * t, qi % ATT_SLOTS
        s = [s_ref[slot, r:r + ATT_ROWS, :] for r in range(0, lo, ATT_ROWS)] + [
            jnp.where(diag_visible[r:r + ATT_ROWS], s_ref[slot, lo + r:lo + r + ATT_ROWS, :], -jnp.inf)
            for r in range(0, t, ATT_ROWS)]
        m = jnp.max(_tree(jnp.maximum, s), axis=0, keepdims=True)
        p = [jnp.exp2(blk - m) for blk in s]
        for i, blk in enumerate(p):
            p_ref[slot, i * ATT_ROWS:(i + 1) * ATT_ROWS, :] = blk.astype(BF16)
        return jnp.sum(_tree(jnp.add, p), axis=0, keepdims=True)

    def values(qi, l):
        lo, hi = qi * t, (qi + 1) * t
        acc = _dot(vt_ref[:, 0:hi], p_ref[qi % ATT_SLOTS, 0:hi, :])
        o_ref[lo:hi, :] = (acc / l).T.astype(o_ref.dtype)

    for qi in range(min(ATT_AHEAD, n_q)):
        scores(qi)
    for qi in range(n_q):
        if qi + ATT_AHEAD < n_q:
            scores(qi + ATT_AHEAD)
        values(qi, softmax(qi))


def _attn_prefill(q, k, vt):
    b, s, _ = q.shape
    return pl.pallas_call(
        _attn_prefill_kernel,
        grid=(b, N_HEADS),
        in_specs=[
            pl.BlockSpec((None, s, HEAD_PAD), lambda bi, h: (bi, 0, h)),
            pl.BlockSpec((None, s, HEAD_PAD), lambda bi, h: (bi, 0, h)),
            pl.BlockSpec((None, V_HEAD, s), lambda bi, h: (bi, h, 0)),
        ],
        out_specs=pl.BlockSpec((None, s, V_HEAD), lambda bi, h: (bi, 0, h)),
        out_shape=jax.ShapeDtypeStruct((b, s, N_HEADS * V_HEAD), BF16),
        scratch_shapes=[pltpu.VMEM((ATT_SLOTS, s, ATT_TILE), F32), pltpu.VMEM((ATT_SLOTS, s, ATT_TILE), BF16)],
        compiler_params=_params("parallel", "parallel"),
        name="attn_prefill",
    )(q, k, vt)


def _attn_decode_kernel(q_ref, k_ref, v_ref, o_ref, *, q_start, t_valid):
    s_q, t = q_ref.shape[0], k_ref.shape[0]
    q_chunk = (q_start + lax.broadcasted_iota(jnp.int32, (s_q, t), 0)) // CHUNK
    k_pos = lax.broadcasted_iota(jnp.int32, (s_q, t), 1)
    visible = (k_pos // CHUNK <= q_chunk) & (k_pos < t_valid)
    for hd in range(N_HEADS):
        qk = slice(hd * HEAD_PAD, (hd + 1) * HEAD_PAD)
        vo = slice(hd * V_HEAD, (hd + 1) * V_HEAD)
        s = jnp.where(visible, _scores(q_ref[:, qk], k_ref[:, qk]), -jnp.inf)
        p = jnp.exp2(s - jnp.max(s, axis=1, keepdims=True))
        l = jnp.sum(p, axis=1, keepdims=True)
        o_ref[:, vo] = (_dot(p.astype(BF16), v_ref[:, vo]) / l).astype(o_ref.dtype)


def _attn_decode(q, k, v, q_start, t_valid):
    b, s, _ = q.shape
    t = k.shape[1]
    seq = lambda bi: (bi, 0, 0)
    return pl.pallas_call(
        functools.partial(_attn_decode_kernel, q_start=q_start, t_valid=t_valid),
        grid=(b,),
        in_specs=[
            pl.BlockSpec((None, s, N_HEADS * HEAD_PAD), seq),
            pl.BlockSpec((None, t, N_HEADS * HEAD_PAD), seq),
            pl.BlockSpec((None, t, N_HEADS * V_HEAD), seq),
        ],
        out_specs=pl.BlockSpec((None, s, N_HEADS * V_HEAD), seq),
        out_shape=jax.ShapeDtypeStruct((b, s, N_HEADS * V_HEAD), BF16),
        compiler_params=_params("parallel"),
        name="attn_decode",
    )(q, k, v)


def _router_kernel(x_ref, wh_ref, wl_ref, comb_ref, code_ref, cnt_ref):
    x = x_ref[...]
    xh = x.astype(BF16)
    xl = (x - xh.astype(F32)).astype(BF16)
    wh = wh_ref[...]
    logits = _dot(xh, wh) + (_dot(xl, wh) + _dot(xh, wl_ref[...]))
    tm = logits.shape[0]
    lane = lax.broadcasted_iota(jnp.int32, logits.shape, 1)
    logits = jnp.where(lane < N_EXPERTS, logits, -jnp.inf)
    m1 = jnp.max(logits, axis=1, keepdims=True)
    i1 = jnp.min(jnp.where(logits == m1, lane, LANES), axis=1, keepdims=True)
    rest = jnp.where(lane == i1, -jnp.inf, logits)
    m2 = jnp.max(rest, axis=1, keepdims=True)
    i2 = jnp.min(jnp.where(rest == m2, lane, LANES), axis=1, keepdims=True)
    e2 = jnp.exp(m2 - m1)
    w1 = 1.0 / (1.0 + e2)
    comb_ref[...] = jnp.where(lane == i1, w1, jnp.where(lane == i2, e2 * w1, 0.0))
    chosen = (lane == i1) | (lane == i2)
    onehot = jnp.where(chosen, 1.0, 0.0)
    earlier = lax.broadcasted_iota(jnp.int32, (tm, tm), 0) > lax.broadcasted_iota(jnp.int32, (tm, tm), 1)
    rank = _dot(jnp.where(earlier, 1.0, 0.0).astype(BF16), onehot.astype(BF16))
    code_ref[...] = jnp.where(chosen, rank, -1.0)
    cnt_ref[...] = jnp.broadcast_to(jnp.sum(onehot, axis=0, keepdims=True), cnt_ref.shape)


def _router(x, w_hi, w_lo, tm):
    n = x.shape[0]
    row = lambda i: (i, 0)
    return pl.pallas_call(
        _router_kernel,
        grid=(n // tm,),
        in_specs=[
            pl.BlockSpec((tm, D_MODEL), row),
            pl.BlockSpec(w_hi.shape, lambda i: (0, 0)),
            pl.BlockSpec(w_lo.shape, lambda i: (0, 0)),
        ],
        out_specs=[
            pl.BlockSpec((tm, LANES), row),
            pl.BlockSpec((tm, LANES), row),
            pl.BlockSpec((8, LANES), row),
        ],
        out_shape=[
            jax.ShapeDtypeStruct((n, LANES), F32),
            jax.ShapeDtypeStruct((n, LANES), F32),
            jax.ShapeDtypeStruct((n // tm * 8, LANES), F32),
        ],
        compiler_params=_params("parallel"),
        name="router",
    )(x, w_hi, w_lo)


def _gmlp_kernel(x_ref, win_ref, g_ref, b_ref, mix_ref, bias_ref, wout_ref, ln_g_ref, ln_b_ref,
                 o_ref, v_ref, *, tm, v_rows, v_every):
    x = x_ref[...]
    h = jax.nn.gelu(_dot(x.astype(BF16), win_ref[...]))
    u = h[:, :D_MODEL]
    v = _layer_norm(h[:, D_MODEL:], g_ref[...], b_ref[...])
    vb = v.astype(BF16)
    bias = bias_ref[...]
    gated = []
    for r in range(tm // GM_MIX):
        rows = slice(r * GM_MIX, (r + 1) * GM_MIX)
        cols = []
        for grp in range(GM_GROUPS):
            lanes = slice(grp * GM_GROUP_DIM, (grp + 1) * GM_GROUP_DIM)
            cols.append(_dot(mix_ref[grp], vb[rows, lanes]))
        mixed = jnp.concatenate(cols, axis=1) + bias
        gated.append((u[rows] * mixed).astype(BF16))
    y = _dot(jnp.concatenate(gated, axis=0), wout_ref[...])
    o_ref[...] = _layer_norm(ALPHA * x + y, ln_g_ref[...], ln_b_ref[...])

    @pl.when(pl.program_id(0) % v_every == v_every - 1)
    def _():
        v_ref[...] = v[tm - v_rows:, :]


def _gmlp(x, w, mix, bias, ln_g, ln_b, tm, v_rows, v_every):
    n = x.shape[0]
    row = lambda i: (i, 0)
    const = lambda i: (0, 0)
    n_v = n // (tm * v_every) * v_rows
    return pl.pallas_call(
        functools.partial(_gmlp_kernel, tm=tm, v_rows=v_rows, v_every=v_every),
        grid=(n // tm,),
        in_specs=[
            pl.BlockSpec((tm, D_MODEL), row),
            pl.BlockSpec(w['w_in'].shape, const),
            pl.BlockSpec((1, D_MODEL), const),
            pl.BlockSpec((1, D_MODEL), const),
            pl.BlockSpec(mix.shape, lambda i: (0, 0, 0)),
            pl.BlockSpec(bias.shape, const),
            pl.BlockSpec(w['w_out'].shape, const),
            pl.BlockSpec((1, D_MODEL), const),
            pl.BlockSpec((1, D_MODEL), const),
        ],
        out_specs=[
            pl.BlockSpec((tm, D_MODEL), row),
            pl.BlockSpec((v_rows, D_MODEL), lambda i: (i // v_every, 0)),
        ],
        out_shape=[
            jax.ShapeDtypeStruct((n, D_MODEL), F32),
            jax.ShapeDtypeStruct((n_v, D_MODEL), F32),
        ],
        compiler_params=_params("arbitrary"),
        name="gmlp",
    )(x, w['w_in'], w['g_norm'], w['b_norm'], mix, bias, w['w_out'], ln_g, ln_b)


def _ln_ple(z, g_ref, b_ref, p_ref, wpg_ref, wpp_ref):
    x2 = _layer_norm(z, g_ref[...], b_ref[...])
    gate = jax.nn.sigmoid(_dot(x2.astype(BF16), wpg_ref[...]))
    return x2 + gate * _dot(p_ref[...].astype(BF16), wpp_ref[...])


def _swiglu(xb, wg_ref, wu_ref, wd_ref):
    d_ff = wg_ref.shape[1]
    y = None
    for lo in range(0, d_ff, FF_CHUNK):
        hi = min(lo + FF_CHUNK, d_ff)
        h = jax.nn.silu(_dot(xb, wg_ref[:, lo:hi])) * _dot(xb, wu_ref[:, lo:hi])
        d = _dot(h.astype(BF16), wd_ref[lo:hi, :])
        y = d if y is None else y + d
    return y


def _attn_ffn_kernel(a_ref, x_ref, wo_ref, g1_ref, b1_ref, wg_ref, wu_ref, wd_ref, g_ref, b_ref, p_ref,
                     wpg_ref, wpp_ref, o_ref):
    x1 = _layer_norm(ALPHA * x_ref[...] + _dot(a_ref[...], wo_ref[...]), g1_ref[...], b1_ref[...])
    y = _swiglu(x1.astype(BF16), wg_ref, wu_ref, wd_ref)
    o_ref[...] = _ln_ple(ALPHA * x1 + y, g_ref, b_ref, p_ref, wpg_ref, wpp_ref)


def _resident(shape):
    return pl.BlockSpec(shape, lambda i: (0,) * len(shape), pipeline_mode=pl.Buffered(1))


def _attn_ffn(a, x, w_out, ln1_g, ln1_b, wg, wu, wd, ln_g, ln_b, p, layer, w_ple_gate, w_ple_proj, tm):
    n = x.shape[0]
    row = lambda i: (i, 0)
    const = lambda i: (0, 0)
    return pl.pallas_call(
        _attn_ffn_kernel,
        grid=(n // tm,),
        in_specs=[
            pl.BlockSpec((tm, a.shape[1]), row),
            pl.BlockSpec((tm, D_MODEL), row),
            _resident(w_out.shape),
            pl.BlockSpec((1, D_MODEL), const),
            pl.BlockSpec((1, D_MODEL), const),
            _resident(wg.shape),
            _resident(wu.shape),
            _resident(wd.shape),
            pl.BlockSpec((1, D_MODEL), const),
            pl.BlockSpec((1, D_MODEL), const),
            pl.BlockSpec((None, tm, p.shape[2]), lambda i: (layer, i, 0)),
            _resident(w_ple_gate.shape),
            _resident(w_ple_proj.shape),
        ],
        out_specs=pl.BlockSpec((tm, D_MODEL), row),
        out_shape=jax.ShapeDtypeStruct((n, D_MODEL), F32),
        compiler_params=_params("parallel"),
        name="attn_ffn",
    )(a, x, w_out, ln1_g, ln1_b, wg, wu, wd, ln_g, ln_b, p, w_ple_gate, w_ple_proj)


def _moe_plan(cnt, n_tiles, tm_ffn, n_ffn_tiles):
    cnt = cnt.reshape(n_tiles, 8, LANES)[:, 0, :N_EXPERTS].astype(jnp.int32)
    nch = (cnt + MOE_CHUNK - 1) // MOE_CHUNK
    base = jnp.cumsum(nch, axis=1) - nch
    seg = (cnt + BF16_ROWS - 1) // BF16_ROWS * BF16_ROWS
    content = jnp.sum(seg, axis=0)
    group_pad = (content + MOE_CHUNK + tm_ffn - 1) // tm_ffn * tm_ffn
    group_start = jnp.cumsum(group_pad) - group_pad
    seg_start = group_start[None, :] + jnp.cumsum(seg, axis=0) - seg
    k = jnp.arange(MOE_SLOTS)[None, :, None]
    lo, hi = base[:, None, :], (base + nch)[:, None, :]
    slot_dst = jnp.sum(jnp.where((k >= lo) & (k < hi), seg_start[:, None, :] + (k - lo) * MOE_CHUNK, 0), axis=2)
    tile_end = jnp.cumsum(group_pad // tm_ffn)
    n_used = tile_end[-1]
    tile_id = jnp.minimum(jnp.arange(n_ffn_tiles), n_used - 1)
    tile_expert = jnp.sum(tile_id[:, None] >= tile_end[None, :], axis=1)
    base_rows = jnp.zeros((n_tiles, 8, LANES), F32).at[:, :, :N_EXPERTS].set(
        (base * MOE_CHUNK).astype(F32)[:, None, :]).reshape(n_tiles * 8, LANES)
    flat = lambda a: a.reshape(-1).astype(jnp.int32)
    used_rows = n_used * tm_ffn
    zero_row = jnp.concatenate([group_start + content, used_rows[None]])
    zero_cnt = jnp.concatenate([group_pad - content, (n_ffn_tiles * tm_ffn - used_rows)[None]]) // BF16_ROWS
    return dict(slot_dst=flat(slot_dst), n_chunks=flat(jnp.sum(nch, axis=1)), base_rows=base_rows,
                zero_row=flat(zero_row), zero_cnt=flat(zero_cnt),
                tile_expert=flat(tile_expert), n_used=flat(n_used))


def _staged_rows(code, base_rows):
    pos = jnp.where(code >= 0.0, base_rows + code, -1.0)
    first = jnp.max(pos, axis=1, keepdims=True)
    second = jnp.max(jnp.where(pos == first, -1.0, pos), axis=1, keepdims=True)
    return pos, first, second


def _chunk_copies(hbm_ref, vmem_ref, sem, dst_ref, nck_ref, tile, slot, to_hbm, action):
    def body(k, carry):
        stage = vmem_ref.at[slot, pl.ds(pl.multiple_of(k * MOE_CHUNK, MOE_CHUNK), MOE_CHUNK)]
        group = hbm_ref.at[pl.ds(pl.multiple_of(dst_ref[tile * MOE_SLOTS + k], BF16_ROWS), MOE_CHUNK)]
        src, dst = (stage, group) if to_hbm else (group, stage)
        copy = pltpu.make_async_copy(src, dst, sem.at[slot])
        copy.start() if action == "start" else copy.wait()
        return carry

    lax.fori_loop(0, nck_ref[tile], body, 0)


def _zero_fill(xs_ref, zero_ref, sem, zrow_ref, zcnt_ref, action):
    for e in range(N_EXPERTS + 1):
        def body(k, carry):
            rows = xs_ref.at[pl.ds(pl.multiple_of(zrow_ref[e] + k * BF16_ROWS, BF16_ROWS), BF16_ROWS)]
            copy = pltpu.make_async_copy(zero_ref, rows, sem.at[2])
            copy.start() if action == "start" else copy.wait()
            return carry

        lax.fori_loop(0, zcnt_ref[e], body, 0)


def _dispatch_kernel(dst_ref, nck_ref, zrow_ref, zcnt_ref, x_ref, code_ref, brow_ref, xs_ref,
                     stage_ref, zero_ref, sem):
    t = pl.program_id(0)
    slot = t % 2
    _, first, second = _staged_rows(code_ref[...], brow_ref[0:1, :])
    lane = lax.broadcasted_iota(jnp.int32, (TOKEN_TILE, LANES), 1)
    rows_t = jnp.where(lane == 0, first, jnp.where(lane == 1, second, -1.0)).T
    r1, r2 = rows_t[0:1, :], rows_t[1:2, :]
    xb = x_ref[...].astype(BF16)
    n_chunks = nck_ref[t]

    def stage(lo, rows):
        row = (lax.broadcasted_iota(jnp.int32, (rows, TOKEN_TILE), 0) + lo).astype(F32)
        sel = jnp.where((row == r1) | (row == r2), 1.0, 0.0).astype(BF16)
        stage_ref[slot, lo:lo + rows, :] = _dot(sel, xb).astype(BF16)

    stage(0, MOE_ALWAYS)
    for g in range(MOE_ALWAYS // MOE_GROUP, MOE_STAGE // MOE_GROUP):
        @pl.when(g * (MOE_GROUP // MOE_CHUNK) < n_chunks)
        def _():
            stage(g * MOE_GROUP, MOE_GROUP)

    copies = functools.partial(_chunk_copies, xs_ref, stage_ref, sem, dst_ref, nck_ref, to_hbm=True)

    @pl.when(t > 0)
    def _():
        copies(tile=t - 1, slot=1 - slot, action="wait")

    copies(tile=t, slot=slot, action="start")

    @pl.when(t == pl.num_programs(0) - 1)
    def _():
        copies(tile=t, slot=slot, action="wait")
        zero_ref[...] = jnp.zeros(zero_ref.shape, BF16)
        _zero_fill(xs_ref, zero_ref, sem, zrow_ref, zcnt_ref, "start")
        _zero_fill(xs_ref, zero_ref, sem, zrow_ref, zcnt_ref, "wait")


def _dispatch(x, code, plan, n_rows):
    n = x.shape[0]
    row = lambda i, *_: (i, 0)
    grid_spec = pltpu.PrefetchScalarGridSpec(
        num_scalar_prefetch=4,
        grid=(n // TOKEN_TILE,),
        in_specs=[
            pl.BlockSpec((TOKEN_TILE, D_MODEL), row),
            pl.BlockSpec((TOKEN_TILE, LANES), row),
            pl.BlockSpec((8, LANES), row),
        ],
        out_specs=pl.BlockSpec(memory_space=pl.ANY),
        scratch_shapes=[pltpu.VMEM((2, MOE_STAGE, D_MODEL), BF16), pltpu.VMEM((BF16_ROWS, D_MODEL), BF16),
                        pltpu.SemaphoreType.DMA((3,))],
    )
    return pl.pallas_call(
        _dispatch_kernel,
        grid_spec=grid_spec,
        out_shape=jax.ShapeDtypeStruct((n_rows, D_MODEL), BF16),
        compiler_params=_params("arbitrary"),
        name="moe_dispatch",
    )(plan['slot_dst'], plan['n_chunks'], plan['zero_row'], plan['zero_cnt'], x, code, plan['base_rows'])


def _moe_ffn_kernel(te_ref, nu_ref, x_ref, wg_ref, wu_ref, wd_ref, o_ref, acc_ref):
    del te_ref
    f = pl.program_id(1)
    used = pl.program_id(0) < nu_ref[0]

    @pl.when(jnp.logical_not(used) & (f == 0))
    def _():
        o_ref[...] = jnp.zeros(o_ref.shape, o_ref.dtype)

    @pl.when(used)
    def _():
        @pl.when(f == 0)
        def _():
            acc_ref[...] = jnp.zeros(acc_ref.shape, F32)

        acc_ref[...] += _swiglu(x_ref[...], wg_ref, wu_ref, wd_ref)

        @pl.when(f == pl.num_programs(1) - 1)
        def _():
            o_ref[...] = acc_ref[...].astype(o_ref.dtype)


def _moe_ffn(xs, plan, wg, wu, wd, layer, tm, tf):
    n_rows = xs.shape[0]
    n_f = wg.shape[-1] // tf
    tile = lambda i, f, te, nu: (jnp.minimum(i, nu[0] - 1), 0)
    f_idx = lambda i, f, nu: jnp.where(i < nu[0], f, n_f - 1)
    grid_spec = pltpu.PrefetchScalarGridSpec(
        num_scalar_prefetch=2,
        grid=(n_rows // tm, n_f),
        in_specs=[
            pl.BlockSpec((tm, D_MODEL), tile),
            pl.BlockSpec((None, None, D_MODEL, tf), lambda i, f, te, nu: (layer, te[i], 0, f_idx(i, f, nu))),
            pl.BlockSpec((None, None, D_MODEL, tf), lambda i, f, te, nu: (layer, te[i], 0, f_idx(i, f, nu))),
            pl.BlockSpec((None, None, tf, D_MODEL), lambda i, f, te, nu: (layer, te[i], f_idx(i, f, nu), 0)),
        ],
        out_specs=pl.BlockSpec((tm, D_MODEL), lambda i, f, te, nu: (i, 0)),
        scratch_shapes=[pltpu.VMEM((tm, D_MODEL), F32)],
    )
    return pl.pallas_call(
        _moe_ffn_kernel,
        grid_spec=grid_spec,
        out_shape=jax.ShapeDtypeStruct((n_rows, D_MODEL), BF16),
        compiler_params=_params("arbitrary", "arbitrary"),
        name="moe_ffn",
    )(plan['tile_expert'], plan['n_used'], xs, wg, wu, wd)


def _combine_kernel(dst_ref, nck_ref, x_ref, comb_ref, code_ref, brow_ref, ys_ref,
                    g_ref, b_ref, p_ref, wpg_ref, wpp_ref, o_ref, buf_ref, y_ref, sem):
    t = pl.program_id(0)
    slot = t % 2
    copies = functools.partial(_chunk_copies, ys_ref, buf_ref, sem, dst_ref, nck_ref, to_hbm=False)

    @pl.when(t == 0)
    def _():
        buf_ref[...] = jnp.zeros(buf_ref.shape, BF16)
        copies(tile=t, slot=slot, action="start")

    @pl.when(t + 1 < pl.num_programs(0))
    def _():
        copies(tile=t + 1, slot=1 - slot, action="start")

    copies(tile=t, slot=slot, action="wait")

    comb = comb_ref[...]
    pos, first, second = _staged_rows(code_ref[...], brow_ref[0:1, :])
    w_first = jnp.max(jnp.where(pos == first, comb, 0.0), axis=1, keepdims=True)
    w_second = jnp.max(jnp.where(pos == second, comb, 0.0), axis=1, keepdims=True)
    n_chunks = nck_ref[t]

    def gathered(lo, n_rows):
        row = (lax.broadcasted_iota(jnp.int32, (TOKEN_TILE, n_rows), 1) + lo).astype(F32)
        rows = buf_ref[slot, lo:lo + n_rows, :]
        return (w_first * _dot(jnp.where(row == first, 1.0, 0.0).astype(BF16), rows)
                + w_second * _dot(jnp.where(row == second, 1.0, 0.0).astype(BF16), rows))

    y_ref[...] = gathered(0, MOE_ALWAYS)
    for g in range(MOE_ALWAYS // MOE_GROUP, MOE_STAGE // MOE_GROUP):
        @pl.when(g * (MOE_GROUP // MOE_CHUNK) < n_chunks)
        def _():
            y_ref[...] += gathered(g * MOE_GROUP, MOE_GROUP)

    o_ref[...] = _ln_ple(ALPHA * x_ref[...] + y_ref[...], g_ref, b_ref, p_ref, wpg_ref, wpp_ref)


def _combine(x, comb, code, plan, ys, ln_g, ln_b, p, layer, w_ple_gate, w_ple_proj):
    n = x.shape[0]
    row = lambda i, *_: (i, 0)
    const = lambda i, *_: (0, 0)
    grid_spec = pltpu.PrefetchScalarGridSpec(
        num_scalar_prefetch=2,
        grid=(n // TOKEN_TILE,),
        in_specs=[
            pl.BlockSpec((TOKEN_TILE, D_MODEL), row),
            pl.BlockSpec((TOKEN_TILE, LANES), row),
            pl.BlockSpec((TOKEN_TILE, LANES), row),
            pl.BlockSpec((8, LANES), row),
            pl.BlockSpec(memory_space=pl.ANY),
            pl.BlockSpec((1, D_MODEL), const),
            pl.BlockSpec((1, D_MODEL), const),
            pl.BlockSpec((None, TOKEN_TILE, p.shape[2]), lambda i, *_: (layer, i, 0)),
            pl.BlockSpec(w_ple_gate.shape, const),
            pl.BlockSpec(w_ple_proj.shape, const),
        ],
        out_specs=pl.BlockSpec((TOKEN_TILE, D_MODEL), row),
        scratch_shapes=[pltpu.VMEM((2, MOE_STAGE, D_MODEL), BF16), pltpu.VMEM((TOKEN_TILE, D_MODEL), F32),
                        pltpu.SemaphoreType.DMA((2,))],
    )
    return pl.pallas_call(
        _combine_kernel,
        grid_spec=grid_spec,
        out_shape=jax.ShapeDtypeStruct((n, D_MODEL), F32),
        compiler_params=_params("arbitrary"),
        name="moe_combine",
    )(plan['slot_dst'], plan['n_chunks'], x, comb, code, plan['base_rows'], ys,
      ln_g, ln_b, p, w_ple_gate, w_ple_proj)


def _moe(x, comb, code, cnt, wg, wu, wd, moe_layer, ln_g, ln_b, p, layer, w_ple_gate, w_ple_proj):
    n = x.shape[0]
    n_tiles = n // TOKEN_TILE
    tm_ffn = 1024 if n >= 8192 else 256
    max_rows = TOP_K * n + n_tiles * N_EXPERTS * (BF16_ROWS - 1) + N_EXPERTS * (MOE_CHUNK + tm_ffn - 1)
    n_ffn_tiles = -(-max_rows // tm_ffn)
    plan = _moe_plan(cnt, n_tiles, tm_ffn, n_ffn_tiles)
    xs = _dispatch(x, code, plan, n_ffn_tiles * tm_ffn)
    ys = _moe_ffn(xs, plan, wg, wu, wd, moe_layer, tm_ffn, MOE_FF_BLOCK)
    return _combine(x, comb, code, plan, ys, ln_g, ln_b, p, layer, w_ple_gate, w_ple_proj)


def _rope_tables(pos):
    inv = ROPE_THETA ** (-jnp.arange(ROPE_HALF, dtype=F32) / ROPE_HALF)
    ang = pos.astype(F32)[:, None] * inv[None, :]
    cos, sin, zero = jnp.cos(ang), jnp.sin(ang), jnp.zeros_like(ang)
    return (jnp.concatenate([cos, cos, zero, zero], axis=1),
            jnp.concatenate([-sin, sin, zero, zero], axis=1))


def _prep_mla(w_in, g_q, w_q_up, g_kv, w_kv_up, w_out):
    k_r = w_in[:, Q_LORA + KV_LORA:]
    w_in_p = jnp.concatenate([w_in[:, :Q_LORA + KV_LORA], k_r, k_r], axis=1)
    wq = w_q_up.reshape(Q_LORA, N_HEADS, QK_NOPE + QK_ROPE)
    wq_p = jnp.concatenate([wq, wq[:, :, QK_NOPE:]], axis=2).reshape(Q_LORA, N_HEADS * HEAD_PAD)
    wkv = w_kv_up.reshape(KV_LORA, N_HEADS, QK_NOPE + V_HEAD)
    w_k = wkv[:, :, :QK_NOPE].reshape(KV_LORA, -1).astype(BF16)
    w_v = wkv[:, :, QK_NOPE:].reshape(KV_LORA, -1).astype(BF16)
    return dict(w_in=w_in_p.astype(BF16), g_q=g_q[None, :], w_q_up=wq_p.astype(BF16), g_kv=g_kv[None, :],
                w_k=w_k, w_v=w_v, w_v_t=w_v.T, w_out=w_out.astype(BF16))


def _prep_gmlp_mix(w_s, b_s, chunk):
    wm = jnp.tril(w_s[:, :chunk, :chunk])
    reps = GM_MIX // chunk
    mix = jnp.einsum('ab,gts->gatbs', jnp.eye(reps, dtype=F32), wm).reshape(GM_GROUPS, GM_MIX, GM_MIX)
    bias = jnp.repeat(jnp.tile(b_s[:, :chunk].T, (reps, 1)), GM_GROUP_DIM, axis=1)
    return mix.astype(BF16), bias


def _row(v):
    return v[None, :]


def _trunk(x, p, past_latent, past_krope, W):
    b, s, _ = x.shape
    n = b * s
    tm = TOKEN_TILE
    past = 0 if past_latent is None else past_latent.shape[2]
    cos_t, sin_t = _rope_tables(past + jnp.arange(s))
    if s < tm:
        cos_t, sin_t = jnp.tile(cos_t, (tm // s, 1)), jnp.tile(sin_t, (tm // s, 1))
    x = x.reshape(n, D_MODEL)
    p = p.reshape(DEPTH, n, -1)
    chunk = min(s, GM_CHUNK)
    new_lat, new_kr, new_v = [], [], []
    for i in range(DEPTH):
        j = i // 2
        ln1 = (_row(W['ln1_g'][i]), _row(W['ln1_b'][i]))
        tail = (_row(W['ln2_g'][i]), _row(W['ln2_b'][i]), p, i, W['w_ple_gate'][i], W['w_ple_proj'][i])
        if i % 2 == 0:
            w = W['mla'][j]
            if past_latent is None:
                q, lat, kr, k, vt = _mla_proj(x, w, cos_t, sin_t, tm, True)
                att = _attn_prefill(q.reshape(b, s, -1), k.reshape(b, s, -1), vt)
                lat = lat.reshape(b, s, KV_LORA)
            else:
                q, lat, kr, krp = _mla_proj(x, w, cos_t, sin_t, tm, False)
                q = q.reshape(b, s, -1)
                lat = lat.reshape(b, s, KV_LORA)
                krp = krp.reshape(b, s, LANES)
                t_valid = past + s
                t_pad = -(-t_valid // LANES) * LANES
                lat_all = jnp.concatenate(
                    [past_latent[j], lat, jnp.zeros((b, t_pad - t_valid, KV_LORA), F32)], axis=1)
                krp_all = jnp.concatenate(
                    [jnp.pad(past_krope[j], ((0, 0), (0, 0), (0, LANES - QK_ROPE))), krp,
                     jnp.zeros((b, t_pad - t_valid, LANES), F32)], axis=1)
                k, v = _kv_up(lat_all, krp_all, w, t_pad)
                att = _attn_decode(q, k, v, past, t_valid)
            new_lat.append(lat)
            new_kr.append(kr.reshape(b, s, QK_ROPE))
            x = _attn_ffn(att.reshape(n, -1), x, w['w_out'], *ln1,
                          W['w_ffn_gate'][j], W['w_ffn_up'][j], W['w_ffn_down'][j], *tail, tm)
        else:
            w = W['gm'][j]
            mix, bias = _prep_gmlp_mix(w['w_spatial'], w['b_spatial'], chunk)
            v_rows, v_every = (GM_CHUNK, s // tm) if s >= GM_CHUNK else (tm, 1)
            x, v_new = _gmlp(x, w, mix, bias, *ln1, tm, v_rows, v_every)
            new_v.append(v_new.reshape(b, chunk, D_MODEL))
            comb, code, cnt = _router(x, W['w_router_hi'][j], W['w_router_lo'][j], tm)
            x = _moe(x, comb, code, cnt, W['w_moe_gate'], W['w_moe_up'], W['w_moe_down'], j, *tail)
    return x.reshape(b, s, D_MODEL), jnp.stack(new_lat), jnp.stack(new_kr), jnp.stack(new_v)


def kernel(x_prompt, x_sample, cache_mla_latent, cache_mla_krope, p_prompt, p_sample, w_mla_in, g_q_norm, w_q_up, g_kv_norm, w_kv_up, w_mla_out, w_gm_in, g_gm_norm, b_gm_norm, w_gm_spatial, b_gm_spatial, w_gm_out, w_ffn_gate, w_ffn_up, w_ffn_down, w_router, w_moe_gate, w_moe_up, w_moe_down, ln1_g, ln1_b, ln2_g, ln2_b, w_ple_gate, w_ple_proj):
    n_mla, n_gm = w_mla_in.shape[0], w_gm_in.shape[0]
    w_router = jnp.pad(w_router, ((0, 0), (0, 0), (0, LANES - N_EXPERTS)))
    w_router_hi = w_router.astype(BF16)
    W = dict(
        mla=[_prep_mla(w_mla_in[j], g_q_norm[j], w_q_up[j], g_kv_norm[j], w_kv_up[j], w_mla_out[j])
             for j in range(n_mla)],
        gm=[dict(w_in=w_gm_in[j].astype(BF16), g_norm=_row(g_gm_norm[j]), b_norm=_row(b_gm_norm[j]),
                 w_spatial=w_gm_spatial[j], b_spatial=b_gm_spatial[j], w_out=w_gm_out[j].astype(BF16))
            for j in range(n_gm)],
        w_ffn_gate=w_ffn_gate.astype(BF16), w_ffn_up=w_ffn_up.astype(BF16), w_ffn_down=w_ffn_down.astype(BF16),
        w_router_hi=w_router_hi, w_router_lo=(w_router - w_router_hi.astype(F32)).astype(BF16),
        w_moe_gate=w_moe_gate.astype(BF16), w_moe_up=w_moe_up.astype(BF16), w_moe_down=w_moe_down.astype(BF16),
        ln1_g=ln1_g, ln1_b=ln1_b, ln2_g=ln2_g, ln2_b=ln2_b,
        w_ple_gate=w_ple_gate.astype(BF16), w_ple_proj=w_ple_proj.astype(BF16),
    )
    y_p, lat_p, kr_p, v_p = _trunk(x_prompt, p_prompt, None, None, W)
    y_s, lat_s, kr_s, v_s = _trunk(x_sample, p_sample, cache_mla_latent, cache_mla_krope, W)
    return (y_p, y_s, lat_p, kr_p, lat_s, kr_s, v_p, v_s)
```
